```python
import math
import jax, jax.numpy as jnp
from jax import lax
import numpy as np

D_MODEL = 2048
BATCH = 8
SEQ = 4096
DEPTH = 4

GRID_W = 64
CTX_LEN = 256
MIX_W = D_MODEL
GROUP_W = MIX_W // 4
HEAD_DIM = 128
ATT_Q_HEADS = GROUP_W // HEAD_DIM
ATT_KV_HEADS = ATT_Q_HEADS // 2
ATT_REP = ATT_Q_HEADS // ATT_KV_HEADS
Q_BLOCK = 128
ROPE_THETA = 10000.0
ROPE_PAIRS_PER_AXIS = HEAD_DIM // 4
CONV_W = 4
CONV_PAD = (2, 1)
LRU_W = GROUP_W
LRU_BLOCKS = 4
LRU_BLOCK_W = LRU_W // LRU_BLOCKS
LRU_C = 8.0
FN_W = GROUP_W
FN_GROUPS = 4
FN_GROUP_W = FN_W // FN_GROUPS
SSD_W = GROUP_W
SSD_HEAD_DIM = 64
SSD_HEADS = SSD_W // SSD_HEAD_DIM
SSD_GROUPS = 2
SSD_HPG = SSD_HEADS // SSD_GROUPS
SSD_STATE = 128
SSD_CHUNK = 128
SSD_CONV_CH = SSD_W + 2 * SSD_GROUPS * SSD_STATE
N_EXPERTS = 32
TOP_K = 4
EXPERT_FF = 768
SWIGLU_LIMIT = 7.0
SWIGLU_ALPHA = 1.702
MOE_BLOCK = 128
EPS = 1e-6
IN_SPLIT_SIZES = (ATT_Q_HEADS * HEAD_DIM, ATT_KV_HEADS * HEAD_DIM, ATT_KV_HEADS * HEAD_DIM,
                  LRU_W, LRU_W, FN_W, SSD_W, SSD_CONV_CH, 2 * SSD_HEADS)
D_IN = sum(IN_SPLIT_SIZES)
F32 = jnp.float32

kernel_name = 'hymba_style_hybrid_dit_moe'


def _rmsnorm(x, g):
    xf = x.astype(F32)
    y = xf * lax.rsqrt(jnp.mean(xf * xf, axis=-1, keepdims=True) + EPS)
    return (y * g.astype(F32)).astype(x.dtype)


def _modulate(h, shift, scale):
    return h * (1 + scale) + shift


def _rev(t, d):
    return t[:, ::-1] if d == 1 else t


def _axial_rope_tables(n_tokens):
    rows = n_tokens // GRID_W
    row = jnp.repeat(jnp.arange(rows), GRID_W).astype(F32)
    col = jnp.tile(jnp.arange(GRID_W), rows).astype(F32)
    inv = ROPE_THETA ** (-jnp.arange(ROPE_PAIRS_PER_AXIS, dtype=F32) / ROPE_PAIRS_PER_AXIS)
    ang = jnp.concatenate([row[:, None] * inv, col[:, None] * inv], axis=-1)
    ang = jnp.concatenate([ang, ang], axis=-1)
    return jnp.cos(ang), jnp.sin(ang)


def _apply_rope(x, cos, sin):
    x1, x2 = jnp.split(x, 2, axis=-1)
    rot = jnp.concatenate([-x2, x1], axis=-1)
    return (x.astype(F32) * cos[:, None] + rot.astype(F32) * sin[:, None]).astype(x.dtype)


def _dwconv(u, w, b):
    y = lax.conv_general_dilated(u, w[:, None, :].astype(u.dtype), window_strides=(1,),
                                 padding=(CONV_PAD,), dimension_numbers=('NWC', 'WIO', 'NWC'),
                                 feature_group_count=u.shape[-1])
    return y + b.astype(u.dtype)


def _attention(q, k, v, q_c, k_c, v_c, q_norm, k_norm, cos, sin):
    B, L, _ = q.shape

    def heads(t, n):
        return t.reshape(t.shape[0], t.shape[1], n, HEAD_DIM)

    q = _apply_rope(_rmsnorm(heads(q, ATT_Q_HEADS), q_norm), cos, sin)
    k = _apply_rope(_rmsnorm(heads(k, ATT_KV_HEADS), k_norm), cos, sin)
    v = heads(v, ATT_KV_HEADS)
    q_c = _rmsnorm(heads(q_c, ATT_Q_HEADS), q_norm)
    k_c = _rmsnorm(heads(k_c, ATT_KV_HEADS), k_norm)
    v_c = heads(v_c, ATT_KV_HEADS)
    scale = HEAD_DIM ** -0.5

    def attend(qb, keys, vals):
        qb = qb.reshape(qb.shape[0], qb.shape[1], ATT_KV_HEADS, ATT_REP, HEAD_DIM)
        s = jnp.einsum('bqkrd,bskd->bkrqs', qb, keys).astype(F32) * scale
        p = jax.nn.softmax(s, axis=-1).astype(vals.dtype)
        o = jnp.einsum('bkrqs,bskd->bqkrd', p, vals)
        return o.reshape(o.shape[0], o.shape[1], ATT_Q_HEADS * HEAD_DIM)

    y_c = attend(q_c, k_c, v_c)
    k_all = jnp.concatenate([k_c, k], axis=1)
    v_all = jnp.concatenate([v_c, v], axis=1)
    n_blk = L // Q_BLOCK
    q_blocks = jnp.moveaxis(q.reshape(B, n_blk, Q_BLOCK, ATT_Q_HEADS, HEAD_DIM), 1, 0)
    y = lax.map(lambda qb: attend(qb, k_all, v_all), q_blocks)
    y = jnp.moveaxis(y, 0, 1).reshape(B, L, ATT_Q_HEADS * HEAD_DIM)
    return y_c, y


def _lru_coeffs(u, wa, ba, wx, bx, lam):
    ub = u.reshape(u.shape[0], u.shape[1], LRU_BLOCKS, LRU_BLOCK_W)
    r = jax.nn.sigmoid(jnp.einsum('blnc,ncd->blnd', ub, wa.astype(F32)).reshape(u.shape) + ba.astype(F32))
    i = jax.nn.sigmoid(jnp.einsum('blnc,ncd->blnd', ub, wx.astype(F32)).reshape(u.shape) + bx.astype(F32))
    log_a = -LRU_C * r * jax.nn.softplus(-lam.astype(F32))
    a = jnp.exp(log_a)
    b = jnp.sqrt(-jnp.expm1(2.0 * log_a)) * (i * u)
    return a, b


def _linear_scan(a, b, h0):
    def combine(left, right):
        a_l, b_l = left
        a_r, b_r = right
        return a_l * a_r, a_r * b_l + b_r

    a_cum, b_cum = lax.associative_scan(combine, (a, b), axis=1)
    return b_cum + a_cum * h0[:, None, :]


def _rglru_bidir(u_c, u, wa, ba, wx, bx, lam):
    u_c32, u32 = u_c.astype(F32), u.astype(F32)
    outs_c, outs = [], []
    for d in range(2):
        a_c, b_c = _lru_coeffs(_rev(u_c32, d), wa[d], ba[d], wx[d], bx[d], lam[d])
        h_c = _linear_scan(a_c, b_c, jnp.zeros_like(u_c32[:, 0]))
        a_l, b_l = _lru_coeffs(_rev(u32, d), wa[d], ba[d], wx[d], bx[d], lam[d])
        h = _linear_scan(a_l, b_l, h_c[:, -1])
        outs_c.append(_rev(h_c, d))
        outs.append(_rev(h, d))
    return (outs_c[0] + outs_c[1]).astype(u_c.dtype), (outs[0] + outs[1]).astype(u.dtype)


def _fourier(u):
    B, L, _ = u.shape
    uf = u.astype(F32).reshape(B, L, FN_GROUPS, FN_GROUP_W)
    y = jnp.fft.fft2(uf, axes=(1, 3), norm='ortho').real
    return y.reshape(B, L, FN_W).astype(u.dtype)


def _ssd_chunked(x, dt, A, Bm, Cm, h0):
    b, L, g, r, p = x.shape
    n = Bm.shape[-1]
    c = L // SSD_CHUNK
    xdt = (x * dt[..., None]).reshape(b, c, SSD_CHUNK, g, r, p)
    Bc = Bm.reshape(b, c, SSD_CHUNK, g, n)
    Cc = Cm.reshape(b, c, SSD_CHUNK, g, n)
    a = jnp.transpose((dt * A).reshape(b, c, SSD_CHUNK, g, r), (0, 3, 4, 1, 2))
    acs = jnp.cumsum(a, axis=-1)
    tril = jnp.tril(jnp.ones((SSD_CHUNK, SSD_CHUNK), dtype=bool))
    seg = acs[..., :, None] - acs[..., None, :]
    Lmat = jnp.exp(jnp.where(tril, seg, -jnp.inf))
    y_diag = jnp.einsum('bclgn,bcsgn,bgrcls,bcsgrp->bclgrp', Cc, Bc, Lmat, xdt)
    decay_states = jnp.exp(acs[..., -1:] - acs)
    states = jnp.einsum('bclgn,bgrcl,bclgrp->bcgrpn', Bc, decay_states, xdt)
    states = jnp.concatenate([h0[:, None], states], axis=1)
    tot = jnp.concatenate([jnp.zeros_like(acs[..., :1, -1]), acs[..., -1]], axis=-1)
    ccs = jnp.cumsum(tot, axis=-1)
    tril_c = jnp.tril(jnp.ones((c + 1, c + 1), dtype=bool))
    dchunk = jnp.exp(jnp.where(tril_c, ccs[..., :, None] - ccs[..., None, :], -jnp.inf))
    new_states = jnp.einsum('bgrzc,bcgrpn->bzgrpn', dchunk, states)
    states_in, final = new_states[:, :-1], new_states[:, -1]
    y_off = jnp.einsum('bclgn,bcgrpn,bgrcl->bclgrp', Cc, states_in, jnp.exp(acs))
    return (y_diag + y_off).reshape(b, L, g, r, p), final


def _ssd_split(xbc, dt):
    B, L = xbc.shape[:2]
    xs, bm, cm = jnp.split(xbc.astype(F32), [SSD_W, SSD_W + SSD_GROUPS * SSD_STATE], axis=-1)
    xs = xs.reshape(B, L, SSD_GROUPS, SSD_HPG, SSD_HEAD_DIM)
    bm = bm.reshape(B, L, SSD_GROUPS, SSD_STATE)
    cm = cm.reshape(B, L, SSD_GROUPS, SSD_STATE)
    dt = dt.astype(F32).reshape(B, L, 2, SSD_GROUPS, SSD_HPG)
    return xs, bm, cm, dt


def _ssd_bidir(xbc_c, dt_c, z_c, xbc, dt, z, a_log, dt_bias, d_skip, norm_g):
    xs_c, b_c, c_c, dr_c = _ssd_split(xbc_c, dt_c)
    xs, b_l, c_l, dr = _ssd_split(xbc, dt)
    dsk = d_skip.astype(F32).reshape(SSD_GROUPS, SSD_HPG, 1)
    ys_c, ys = [xs_c * dsk], [xs * dsk]
    for d in range(2):
        A = -jnp.exp(a_log[d].astype(F32)).reshape(SSD_GROUPS, SSD_HPG)
        bias = dt_bias[d].astype(F32).reshape(SSD_GROUPS, SSD_HPG)
        dtc = jax.nn.softplus(dr_c[:, :, d] + bias)
        dtl = jax.nn.softplus(dr[:, :, d] + bias)
        h0 = jnp.zeros((xs_c.shape[0], SSD_GROUPS, SSD_HPG, SSD_HEAD_DIM, SSD_STATE), F32)
        y_cd, h_ctx = _ssd_chunked(_rev(xs_c, d), _rev(dtc, d), A, _rev(b_c, d), _rev(c_c, d), h0)
        y_ld, _ = _ssd_chunked(_rev(xs, d), _rev(dtl, d), A, _rev(b_l, d), _rev(c_l, d), h_ctx)
        ys_c.append(_rev(y_cd, d))
        ys.append(_rev(y_ld, d))

    def gated_norm(parts, zz):
        y = (parts[0] + parts[1] + parts[2]).reshape(zz.shape)
        return _rmsnorm(y * jax.nn.silu(zz.astype(F32)), norm_g).astype(zz.dtype)

    return gated_norm(ys_c, z_c), gated_norm(ys, z)


def _token_mixers(u_c, u, cos, sin, q_norm, k_norm, lru_conv_w, lru_conv_b, lru_wa, lru_ba,
                  lru_wx, lru_bx, lru_lam, ssd_conv_w, ssd_conv_b, ssd_a_log, ssd_dt_bias,
                  ssd_d, ssd_norm):
    cuts = np.cumsum(IN_SPLIT_SIZES)[:-1].tolist()
    q_c, k_c, v_c, lx_c, lg_c, f_c, z_c, xbc_c, dt_c = jnp.split(u_c, cuts, axis=-1)
    q, k, v, lx, lg, f, z, xbc, dt = jnp.split(u, cuts, axis=-1)
    att_c, att = _attention(q, k, v, q_c, k_c, v_c, q_norm, k_norm, cos, sin)
    rec_c, rec = _rglru_bidir(_dwconv(lx_c, lru_conv_w, lru_conv_b), _dwconv(lx, lru_conv_w, lru_conv_b),
                              lru_wa, lru_ba, lru_wx, lru_bx, lru_lam)
    lru_c = rec_c * jax.nn.gelu(lg_c)
    lru = rec * jax.nn.gelu(lg)
    four_c, four = _fourier(f_c), _fourier(f)
    ssd_c, ssd = _ssd_bidir(jax.nn.silu(_dwconv(xbc_c, ssd_conv_w, ssd_conv_b)), dt_c, z_c,
                            jax.nn.silu(_dwconv(xbc, ssd_conv_w, ssd_conv_b)), dt, z,
                            ssd_a_log, ssd_dt_bias, ssd_d, ssd_norm)
    feat_c = jnp.concatenate([att_c, lru_c, four_c, ssd_c], axis=-1)
    feat = jnp.concatenate([att, lru, four, ssd], axis=-1)
    return feat_c, feat


def _moe(h, router_w, router_b, w_gu, b_gu, w_dn, b_dn):
    T = h.shape[0]
    TK = T * TOP_K
    logits = (h @ router_w + router_b).astype(F32)
    top_v, top_i = lax.top_k(logits, TOP_K)
    gates = jax.nn.softmax(top_v, axis=-1)
    flat_e = top_i.reshape(-1)
    flat_t = jnp.repeat(jnp.arange(T, dtype=jnp.int32), TOP_K)
    flat_g = gates.reshape(-1)
    order = jnp.argsort(flat_e)
    se, st, sg = flat_e[order], flat_t[order], flat_g[order]
    counts = jnp.bincount(flat_e, length=N_EXPERTS)
    starts = jnp.cumsum(counts) - counts
    pcounts = (counts + MOE_BLOCK - 1) // MOE_BLOCK * MOE_BLOCK
    pends = jnp.cumsum(pcounts)
    pstarts = pends - pcounts
    dest = pstarts[se] + (jnp.arange(TK) - starts[se])
    n_blocks = -(-TK // MOE_BLOCK) + N_EXPERTS
    P = n_blocks * MOE_BLOCK
    tok = jnp.zeros((P,), jnp.int32).at[dest].set(st)
    gw = jnp.zeros((P,), F32).at[dest].set(sg)
    blk_e = jnp.minimum(jnp.searchsorted(pends, jnp.arange(n_blocks) * MOE_BLOCK, side='right'),
                        N_EXPERTS - 1)

    def run(args):
        e, idx, g = args
        xb = h[idx]
        gu = xb @ w_gu[e] + b_gu[e]
        gate = jnp.minimum(gu[..., ::2], SWIGLU_LIMIT)
        up = jnp.clip(gu[..., 1::2], -SWIGLU_LIMIT, SWIGLU_LIMIT)
        act = gate * jax.nn.sigmoid(SWIGLU_ALPHA * gate) * (up + 1)
        out = act @ w_dn[e] + b_dn[e]
        return out * g.astype(h.dtype)[:, None]

    outs = lax.map(run, (blk_e, tok.reshape(n_blocks, MOE_BLOCK), gw.reshape(n_blocks, MOE_BLOCK)))
    return jnp.zeros_like(h).at[tok].add(outs.reshape(P, h.shape[1]).astype(h.dtype))


def _layer(x, xc, mod, mod_c, cos, sin, update_ctx, norm1, norm2, w_in, w_out, q_norm, k_norm,
           lru_conv_w, lru_conv_b, lru_wa, lru_ba, lru_wx, lru_bx, lru_lam, ssd_conv_w, ssd_conv_b,
           ssd_a_log, ssd_dt_bias, ssd_d, ssd_norm, router_w, router_b, exp_w_gu, exp_b_gu,
           exp_w_dn, exp_b_dn):
    sh1, sc1, g1, sh2, sc2, g2 = jnp.split(mod[:, None, :], 6, axis=-1)
    sh1c, sc1c, g1c, sh2c, sc2c, g2c = jnp.split(mod_c, 6, axis=-1)
    h = _modulate(_rmsnorm(x, norm1), sh1, sc1)
    hc = _modulate(_rmsnorm(xc, norm1), sh1c, sc1c)
    feat_c, feat = _token_mixers(hc @ w_in, h @ w_in, cos, sin, q_norm, k_norm, lru_conv_w, lru_conv_b,
                                 lru_wa, lru_ba, lru_wx, lru_bx, lru_lam, ssd_conv_w, ssd_conv_b,
                                 ssd_a_log, ssd_dt_bias, ssd_d, ssd_norm)
    x = x + g1 * (feat @ w_out)
    tokens = _modulate(_rmsnorm(x, norm2), sh2, sc2).reshape(-1, x.shape[-1])
    if update_ctx:
        xc = xc + g1c * (feat_c @ w_out)
        h2c = _modulate(_rmsnorm(xc, norm2), sh2c, sc2c).reshape(-1, xc.shape[-1])
        n_c = h2c.shape[0]
        ffn = _moe(jnp.concatenate([h2c, tokens], axis=0), router_w, router_b, exp_w_gu, exp_b_gu,
                   exp_w_dn, exp_b_dn)
        xc = xc + g2c * ffn[:n_c].reshape(xc.shape)
        ffn = ffn[n_c:]
    else:
        ffn = _moe(tokens, router_w, router_b, exp_w_gu, exp_b_gu, exp_w_dn, exp_b_dn)
    x = x + g2 * ffn.reshape(x.shape)
    return x, xc


def setup_inputs(seed: int = 0) -> dict:
    key = jax.random.key(seed)
    keys = iter(jax.random.split(key, 31))
    L = DEPTH

    def normal(shape, std):
        return std * jax.random.normal(next(keys), shape, F32)

    def gain(shape):
        return 1.0 + normal(shape, 0.02)

    def uniform(shape, lo, hi):
        return jax.random.uniform(next(keys), shape, F32, lo, hi)

    inp = {}
    inp['x'] = normal((BATCH, SEQ, D_MODEL), 1.0)
    inp['c'] = normal((BATCH, D_MODEL), 1.0)
    inp['ctx'] = normal((BATCH, CTX_LEN, D_MODEL), 1.0)
    inp['c_ctx'] = normal((D_MODEL,), 1.0)
    inp['w_mod'] = normal((L, D_MODEL, 6 * D_MODEL), 0.5 * D_MODEL ** -0.5)
    inp['b_mod'] = normal((L, 6 * D_MODEL), 0.02)
    inp['norm1'] = gain((L, D_MODEL))
    inp['norm2'] = gain((L, D_MODEL))
    inp['w_in'] = normal((L, D_MODEL, D_IN), D_MODEL ** -0.5)
    inp['w_out'] = normal((L, MIX_W, D_MODEL), MIX_W ** -0.5)
    inp['q_norm'] = gain((L, HEAD_DIM))
    inp['k_norm'] = gain((L, HEAD_DIM))
    inp['lru_conv_w'] = normal((L, CONV_W, LRU_W), CONV_W ** -0.5)
    inp['lru_conv_b'] = normal((L, LRU_W), 0.02)
    inp['lru_wa'] = normal((L, 2, LRU_BLOCKS, LRU_BLOCK_W, LRU_BLOCK_W), LRU_BLOCK_W ** -0.5)
    inp['lru_ba'] = normal((L, 2, LRU_W), 0.02)
    inp['lru_wx'] = normal((L, 2, LRU_BLOCKS, LRU_BLOCK_W, LRU_BLOCK_W), LRU_BLOCK_W ** -0.5)
    inp['lru_bx'] = normal((L, 2, LRU_W), 0.02)
    a0 = uniform((L, 2, LRU_W), 0.9, 0.999)
    inp['lru_lam'] = jnp.log(a0) - jnp.log1p(-a0)
    inp['ssd_conv_w'] = normal((L, CONV_W, SSD_CONV_CH), CONV_W ** -0.5)
    inp['ssd_conv_b'] = normal((L, SSD_CONV_CH), 0.02)
    inp['ssd_a_log'] = jnp.log(uniform((L, 2, SSD_HEADS), 1.0, 16.0))
    dt0 = jnp.exp(uniform((L, 2, SSD_HEADS), math.log(1e-3), math.log(1e-1)))
    inp['ssd_dt_bias'] = dt0 + jnp.log(-jnp.expm1(-dt0))
    inp['ssd_d'] = gain((L, SSD_HEADS))
    inp['ssd_norm'] = gain((L, SSD_W))
    inp['router_w'] = normal((L, D_MODEL, N_EXPERTS), D_MODEL ** -0.5)
    inp['router_b'] = normal((L, N_EXPERTS), 0.01)
    inp['exp_w_gu'] = normal((L, N_EXPERTS, D_MODEL, 2 * EXPERT_FF), D_MODEL ** -0.5)
    inp['exp_b_gu'] = normal((L, N_EXPERTS, 2 * EXPERT_FF), 0.02)
    inp['exp_w_dn'] = normal((L, N_EXPERTS, EXPERT_FF, D_MODEL), EXPERT_FF ** -0.5)
    inp['exp_b_dn'] = normal((L, N_EXPERTS, D_MODEL), 0.02)
    return inp


def reference(x, c, ctx, c_ctx, w_mod, b_mod, norm1, norm2, w_in, w_out, q_norm, k_norm,
              lru_conv_w, lru_conv_b, lru_wa, lru_ba, lru_wx, lru_bx, lru_lam, ssd_conv_w,
              ssd_conv_b, ssd_a_log, ssd_dt_bias, ssd_d, ssd_norm, router_w, router_b,
              exp_w_gu, exp_b_gu, exp_w_dn, exp_b_dn):
    cos, sin = _axial_rope_tables(x.shape[1])
    s_c = jax.nn.silu(c)
    s_cc = jax.nn.silu(c_ctx)
    xc = ctx
    for l in range(DEPTH):
        mod = s_c @ w_mod[l] + b_mod[l]
        mod_c = s_cc @ w_mod[l] + b_mod[l]
        x, xc = _layer(x, xc, mod, mod_c, cos, sin, l < DEPTH - 1, norm1[l], norm2[l], w_in[l],
                       w_out[l], q_norm[l], k_norm[l], lru_conv_w[l], lru_conv_b[l], lru_wa[l],
                       lru_ba[l], lru_wx[l], lru_bx[l], lru_lam[l], ssd_conv_w[l], ssd_conv_b[l],
                       ssd_a_log[l], ssd_dt_bias[l], ssd_d[l], ssd_norm[l], router_w[l], router_b[l],
                       exp_w_gu[l], exp_b_gu[l], exp_w_dn[l], exp_b_dn[l])
    return x
```

```python
import functools
import math

import jax
import jax.numpy as jnp
import numpy as np
from jax import lax
from jax.experimental import pallas as pl
from jax.experimental.pallas import tpu as pltpu

F32 = jnp.float32
BF16 = jnp.bfloat16
I32 = jnp.int32

HEAD_DIM = 128
ATT_Q_HEADS = 4
ATT_KV_HEADS = 2
GRID_W = 64
ROPE_THETA = 10000.0
CONV_W = 4
LRU_BLOCKS = 4
LRU_C = 8.0
FN_GROUPS = 4
SSD_HEAD_DIM = 64
SSD_HEADS = 8
SSD_GROUPS = 2
SSD_STATE = 128
SSD_CHUNK = 128
TOP_K = 4
SWIGLU_LIMIT = 7.0
SWIGLU_ALPHA = 1.702
EPS = 1e-6
GROUP_W = 512

LANES = 128
SUBLANES = 8
VMEM_LIMIT = 56 * 1024 * 1024

NEG_BIG = -1e30

TM_INPROJ = 1024
TN_INPROJ = 512
TM_TOKEN = 256
TQ_ATT = 256
TK_ATT = 512
TC_SCAN = 256
TM_DFT = 1024
TK_DFT = 1024
BM_MOE = 512
TM_COMBINE = 128


def _cp(sem, vmem=VMEM_LIMIT):
    return pltpu.CompilerParams(dimension_semantics=sem, vmem_limit_bytes=vmem)


def _dot(a, b):
    return jnp.dot(a, b, preferred_element_type=F32)


def _dot_nt(a, b):
    return lax.dot_general(a, b, (((1,), (1,)), ((), ())), preferred_element_type=F32)


def _dot_tn(a, b):
    return lax.dot_general(a, b, (((0,), (0,)), ((), ())), preferred_element_type=F32)


def _split3(x):
    hi = x.astype(BF16)
    r = x - hi.astype(F32)
    mid = r.astype(BF16)
    lo = (r - mid.astype(F32)).astype(BF16)
    return hi, mid, lo


def _dot_x01(x, m01):
    hi, mid, lo = _split3(x)
    return _dot(hi, m01) + _dot(mid, m01) + _dot(lo, m01)


def _dot_01x(m01, x):
    hi, mid, lo = _split3(x)
    return _dot(m01, hi) + _dot(m01, mid) + _dot(m01, lo)


def _dot_hi(a, b):
    ah = a.astype(BF16)
    al = (a - ah.astype(F32)).astype(BF16)
    bh = b.astype(BF16)
    bl = (b - bh.astype(F32)).astype(BF16)
    return _dot(ah, bh) + _dot(ah, bl) + _dot(al, bh)


def _sigmoid(x):
    return 1.0 / (1.0 + jnp.exp(-x))


def _silu(x):
    return x * _sigmoid(x)


def _softplus(x):
    return jnp.maximum(x, 0.0) + jnp.log(1.0 + jnp.exp(-jnp.abs(x)))


def _gelu_tanh(x):
    c = math.sqrt(2.0 / math.pi)
    return 0.5 * x * (1.0 + jnp.tanh(c * (x + 0.044715 * (x * x * x))))


def _rms(x, g):
    return x * lax.rsqrt(jnp.mean(x * x, axis=-1, keepdims=True) + EPS) * g


def _mod_kernel(cc_ref, w_ref, b_ref, o_ref):
    s = _silu(cc_ref[...])
    o_ref[...] = _dot_hi(s, w_ref[...]) + b_ref[...]


def _mod_all(cc, w_mod, b_mod):
    n_layers, d, d6 = w_mod.shape
    tn = 1024
    rows = cc.shape[0]
    return pl.pallas_call(
        _mod_kernel,
        out_shape=jax.ShapeDtypeStruct((n_layers, rows, d6), F32),
        grid=(n_layers, d6 // tn),
        in_specs=[
            pl.BlockSpec((rows, d), lambda l, j: (0, 0)),
            pl.BlockSpec((None, d, tn), lambda l, j: (l, 0, j)),
            pl.BlockSpec((None, 1, tn), lambda l, j: (l, 0, j)),
        ],
        out_specs=pl.BlockSpec((None, rows, tn), lambda l, j: (l, 0, j)),
        compiler_params=_cp(("arbitrary", "arbitrary")),
        name="mod_all",
    )(cc, w_mod, b_mod.reshape(n_layers, 1, d6))


def _inproj_kernel(x_ref, mod_ref, n1_ref, w_ref, wdt_ref, u_ref, dt_ref, h_scr):
    @pl.when(pl.program_id(1) == 0)
    def _():
        x = x_ref[...]
        h = _rms(x, n1_ref[...]) * (1.0 + mod_ref[1:2, :]) + mod_ref[0:1, :]
        hb = h.astype(BF16)
        h_scr[...] = hb
        dt_ref[...] = _dot(hb, wdt_ref[...])

    u_ref[...] = _dot(h_scr[...], w_ref[...])


def _inproj(x, mod6, norm1, w_main, w_dt, dims):
    t, d = x.shape
    n_main = w_main.shape[1]
    tm = math.gcd(math.gcd(dims["L"], dims["TC"]), TM_INPROJ)
    tn = TN_INPROJ
    mod_row = dims["mod_row"]
    return pl.pallas_call(
        _inproj_kernel,
        out_shape=(jax.ShapeDtypeStruct((t, n_main), F32), jax.ShapeDtypeStruct((t, LANES), F32)),
        grid=(t // tm, n_main // tn),
        in_specs=[
            pl.BlockSpec((tm, d), lambda i, j: (i, 0)),
            pl.BlockSpec((None, 6, d), lambda i, j: (mod_row(i * tm), 0, 0)),
            pl.BlockSpec((1, d), lambda i, j: (0, 0)),
            pl.BlockSpec((d, tn), lambda i, j: (0, j)),
            pl.BlockSpec((d, LANES), lambda i, j: (0, 0)),
        ],
        out_specs=(pl.BlockSpec((tm, tn), lambda i, j: (i, j)),
                   pl.BlockSpec((tm, LANES), lambda i, j: (i, 0))),
        scratch_shapes=[pltpu.VMEM((tm, d), BF16)],
        compiler_params=_cp(("arbitrary", "arbitrary")),
        name="inproj",
    )(x, mod6, norm1.reshape(1, d), w_main, w_dt)


def _qkprep_kernel(q_ref, k_ref, v_ref, cos_ref, sin_ref, qn_ref, kn_ref, qo_ref, ko_ref, vo_ref):
    cos = cos_ref[...]
    sin = sin_ref[...]

    def prep(xh, g):
        y = _rms(xh, g)
        return y * cos + pltpu.roll(y, HEAD_DIM // 2, 1) * sin

    for h in range(ATT_Q_HEADS):
        sl = slice(h * HEAD_DIM, (h + 1) * HEAD_DIM)
        qo_ref[:, sl] = prep(q_ref[:, sl], qn_ref[...]).astype(BF16)
    for h in range(ATT_KV_HEADS):
        sl = slice(h * HEAD_DIM, (h + 1) * HEAD_DIM)
        ko_ref[:, sl] = prep(k_ref[:, sl], kn_ref[...]).astype(BF16)
    vo_ref[...] = v_ref[...].astype(BF16)


def _qkprep(u, cos_t, sin_t, q_norm, k_norm, dims):
    t = u.shape[0]
    tp = math.gcd(math.gcd(dims["L"], dims["C"]), TM_TOKEN)
    nl, seq, ctx = dims["NL"], dims["L"], dims["C"]

    def tab(i):
        r = i * tp
        return jnp.where(r < nl, ctx // tp + (r % seq) // tp, 0)

    qw, kw = ATT_Q_HEADS * HEAD_DIM, ATT_KV_HEADS * HEAD_DIM
    return pl.pallas_call(
        _qkprep_kernel,
        out_shape=(jax.ShapeDtypeStruct((t, qw), BF16), jax.ShapeDtypeStruct((t, kw), BF16),
                   jax.ShapeDtypeStruct((t, kw), BF16)),
        grid=(t // tp,),
        in_specs=[
            pl.BlockSpec((tp, qw), lambda i: (i, 0)),
            pl.BlockSpec((tp, kw), lambda i: (i, qw // kw)),
            pl.BlockSpec((tp, kw), lambda i: (i, qw // kw + 1)),
            pl.BlockSpec((tp, HEAD_DIM), lambda i: (tab(i), 0)),
            pl.BlockSpec((tp, HEAD_DIM), lambda i: (tab(i), 0)),
            pl.BlockSpec((1, HEAD_DIM), lambda i: (0, 0)),
            pl.BlockSpec((1, HEAD_DIM), lambda i: (0, 0)),
        ],
        out_specs=(pl.BlockSpec((tp, qw), lambda i: (i, 0)),
                   pl.BlockSpec((tp, kw), lambda i: (i, 0)),
                   pl.BlockSpec((tp, kw), lambda i: (i, 0))),
        compiler_params=_cp(("arbitrary",)),
        name="qkprep",
    )(u, u, u, cos_t, sin_t, q_norm.reshape(1, HEAD_DIM), k_norm.reshape(1, HEAD_DIM))


def _attn_kernel(q_ref, kc_ref, vc_ref, kl_ref, vl_ref, o_ref, *, n_lat_q, n_lat_chunks, tk):
    tq = q_ref.shape[0]
    scale = HEAD_DIM ** -0.5
    q2 = jnp.concatenate([q_ref[:, :HEAD_DIM], q_ref[:, HEAD_DIM:]], axis=0)

    def step(k, v, carry):
        m, l, acc = carry
        s = _dot_nt(q2, k) * scale
        m_new = jnp.maximum(m, jnp.max(s, axis=-1, keepdims=True))
        alpha = jnp.exp(m - m_new)
        p = jnp.exp(s - m_new)
        l = alpha * l + jnp.sum(p, axis=-1, keepdims=True)
        acc = alpha * acc + _dot(p.astype(BF16), v)
        return m_new, l, acc

    init = (jnp.full((2 * tq, 1), NEG_BIG, F32), jnp.zeros((2 * tq, 1), F32),
            jnp.zeros((2 * tq, HEAD_DIM), F32))
    carry = step(kc_ref[...], vc_ref[...], init)

    def body(j, c):
        off = pl.multiple_of(j * tk, tk)
        return step(kl_ref[pl.ds(off, tk), :], vl_ref[pl.ds(off, tk), :], c)

    n = jnp.where(pl.program_id(2) < n_lat_q, n_lat_chunks, 0)
    m, l, acc = lax.fori_loop(0, n, body, carry)
    o = acc / l
    o_ref[:, :HEAD_DIM] = o[:tq]
    o_ref[:, HEAD_DIM:] = o[tq:]


def _attention(qn, kn, vb, dims):
    t = qn.shape[0]
    b, seq, ctx, nl = dims["B"], dims["L"], dims["C"], dims["NL"]
    tq = math.gcd(math.gcd(seq, ctx), TQ_ATT)
    tk = math.gcd(seq, TK_ATT)
    n_lat_q, n_ctx_q = seq // tq, ctx // tq
    rep = ATT_Q_HEADS // ATT_KV_HEADS

    def qrow(bi, i):
        return jnp.where(i < n_lat_q, bi * n_lat_q + i, nl // tq + bi * n_ctx_q + (i - n_lat_q))

    kernel = functools.partial(_attn_kernel, n_lat_q=n_lat_q, n_lat_chunks=seq // tk, tk=tk)
    return pl.pallas_call(
        kernel,
        out_shape=jax.ShapeDtypeStruct((t, ATT_Q_HEADS * HEAD_DIM), F32),
        grid=(b, ATT_KV_HEADS, n_lat_q + n_ctx_q),
        in_specs=[
            pl.BlockSpec((tq, rep * HEAD_DIM), lambda bi, g, i: (qrow(bi, i), g)),
            pl.BlockSpec((ctx, HEAD_DIM), lambda bi, g, i: (nl // ctx + bi, g)),
            pl.BlockSpec((ctx, HEAD_DIM), lambda bi, g, i: (nl // ctx + bi, g)),
            pl.BlockSpec((seq, HEAD_DIM), lambda bi, g, i: (bi, g)),
            pl.BlockSpec((seq, HEAD_DIM), lambda bi, g, i: (bi, g)),
        ],
        out_specs=pl.BlockSpec((tq, rep * HEAD_DIM), lambda bi, g, i: (qrow(bi, i), g)),
        compiler_params=_cp(("arbitrary", "arbitrary", "arbitrary")),
        name="attention",
    )(qn, kn, vb, kn, vb)


def _dwconv_kernel(x_ref, p_ref, n_ref, w_ref, b_ref, o_ref, *, rows_lat, seq, ctx, act):
    tc = x_ref.shape[0]
    r0 = pl.program_id(0) * tc
    pos = jnp.where(r0 < rows_lat, r0 % seq, (r0 - rows_lat) % ctx)
    length = jnp.where(r0 < rows_lat, seq, ctx)
    has_prev = (pos > 0).astype(F32)
    has_next = (pos + tc < length).astype(F32)
    x = x_ref[...]
    row = lax.broadcasted_iota(I32, x.shape, 0)
    p1 = p_ref[SUBLANES - 1:SUBLANES, :] * has_prev
    p2 = p_ref[SUBLANES - 2:SUBLANES - 1, :] * has_prev
    n0 = n_ref[0:1, :] * has_next
    x1 = jnp.where(row == 0, p1, pltpu.roll(x, 1, 0))
    x2 = jnp.where(row == 0, p2, jnp.where(row == 1, p1, pltpu.roll(x, 2, 0)))
    xn = jnp.where(row == tc - 1, n0, pltpu.roll(x, tc - 1, 0))
    y = w_ref[0:1, :] * x2 + w_ref[1:2, :] * x1 + w_ref[2:3, :] * x + w_ref[3:4, :] * xn + b_ref[...]
    o_ref[...] = _silu(y) if act else y


def _dwconv(u, col_off, width, w, bias, dims, act):
    t = u.shape[0]
    tc = math.gcd(math.gcd(dims["L"], dims["C"]), TM_TOKEN)
    cb = col_off // width
    per = tc // SUBLANES
    last8 = t // SUBLANES - 1
    kernel = functools.partial(_dwconv_kernel, rows_lat=dims["NL"], seq=dims["L"], ctx=dims["C"], act=act)
    return pl.pallas_call(
        kernel,
        out_shape=jax.ShapeDtypeStruct((t, width), F32),
        grid=(t // tc,),
        in_specs=[
            pl.BlockSpec((tc, width), lambda i: (i, cb)),
            pl.BlockSpec((SUBLANES, width), lambda i: (jnp.maximum(i * per - 1, 0), cb)),
            pl.BlockSpec((SUBLANES, width), lambda i: (jnp.minimum((i + 1) * per, last8), cb)),
            pl.BlockSpec((CONV_W, width), lambda i: (0, 0)),
            pl.BlockSpec((1, width), lambda i: (0, 0)),
        ],
        out_specs=pl.BlockSpec((tc, width), lambda i: (i, 0)),
        compiler_params=_cp(("arbitrary",)),
        name="dwconv",
    )(u, u, u, w, bias.reshape(1, width))


def _lru_kernel(*refs, direction, final):
    if final:
        (u_ref, wa_ref, ba_ref, wx_ref, bx_ref, lam_ref, hb_ref, lg_ref, o_ref, h0_scr) = refs
    else:
        (u_ref, wa_ref, ba_ref, wx_ref, bx_ref, lam_ref, o_ref, h0_scr) = refs
    tc, width = u_ref.shape
    bw = width // LRU_BLOCKS

    @pl.when(pl.program_id(1) == 0)
    def _():
        h0_scr[...] = jnp.zeros_like(h0_scr)

    u = u_ref[...]
    ub = u.astype(BF16)
    ga = jnp.concatenate([_dot(ub[:, n * bw:(n + 1) * bw], wa_ref[n]) for n in range(LRU_BLOCKS)], axis=1)
    gx = jnp.concatenate([_dot(ub[:, n * bw:(n + 1) * bw], wx_ref[n]) for n in range(LRU_BLOCKS)], axis=1)
    r = _sigmoid(ga + ba_ref[...])
    i = _sigmoid(gx + bx_ref[...])
    log_a = (-LRU_C) * r * _softplus(-lam_ref[...])
    a = jnp.exp(log_a)
    bv = jnp.sqrt(1.0 - a * a) * (i * u)

    row = lax.broadcasted_iota(I32, a.shape, 0)
    s = 1
    while s < tc:
        if direction == 0:
            keep = row >= s
            a_s = jnp.where(keep, pltpu.roll(a, s, 0), 1.0)
            b_s = jnp.where(keep, pltpu.roll(bv, s, 0), 0.0)
        else:
            keep = row < tc - s
            a_s = jnp.where(keep, pltpu.roll(a, tc - s, 0), 1.0)
            b_s = jnp.where(keep, pltpu.roll(bv, tc - s, 0), 0.0)
        bv = a * b_s + bv
        a = a * a_s
        s *= 2
    h = bv + a * h0_scr[...]
    h0_scr[...] = h[tc - 1:tc, :] if direction == 0 else h[0:1, :]
    if final:
        o_ref[...] = (h + hb_ref[...]) * _gelu_tanh(lg_ref[...])
    else:
        o_ref[...] = h


def _lru_pass(uconv, u_all, lg_col_block, wa, ba, wx, bx, lam, hb, dims, direction):
    t, width = uconv.shape
    b, seq, ctx, nl = dims["B"], dims["L"], dims["C"], dims["NL"]
    tc = math.gcd(math.gcd(seq, ctx), TC_SCAN)
    n_c, n_l = ctx // tc, seq // tc
    final = hb is not None
    bw = width // LRU_BLOCKS

    def blk(bi, s):
        c_idx = s if direction == 0 else n_c - 1 - s
        l_idx = (s - n_c) if direction == 0 else n_l - 1 - (s - n_c)
        return jnp.where(s < n_c, nl // tc + bi * n_c + c_idx, bi * n_l + l_idx)

    row_spec = pl.BlockSpec((tc, width), lambda bi, s: (blk(bi, s), 0))
    vec = pl.BlockSpec((1, width), lambda bi, s: (0, 0))
    wspec = pl.BlockSpec((LRU_BLOCKS, bw, bw), lambda bi, s: (0, 0, 0))
    in_specs = [row_spec, wspec, vec, wspec, vec, vec]
    args = [uconv, wa, ba.reshape(1, width), wx, bx.reshape(1, width), lam.reshape(1, width)]
    if final:
        in_specs += [row_spec, pl.BlockSpec((tc, width), lambda bi, s: (blk(bi, s), lg_col_block))]
        args += [hb, u_all]
    kernel = functools.partial(_lru_kernel, direction=direction, final=final)
    return pl.pallas_call(
        kernel,
        out_shape=jax.ShapeDtypeStruct((t, width), F32),
        grid=(b, n_c + n_l),
        in_specs=in_specs,
        out_specs=row_spec,
        scratch_shapes=[pltpu.VMEM((1, width), F32)],
        compiler_params=_cp(("arbitrary", "arbitrary")),
        name="lru_dir%d" % direction,
    )(*args)


def _chdft_kernel(x_ref, c_ref, s_ref, z_ref):
    gw = c_ref.shape[0]
    n_g = x_ref.shape[1] // gw
    for g in range(n_g):
        xg = x_ref[:, g * gw:(g + 1) * gw].astype(BF16)
        z_ref[:, g * gw:(g + 1) * gw] = _dot(xg, c_ref[...]).astype(BF16)
        z_ref[:, (n_g + g) * gw:(n_g + g + 1) * gw] = _dot(xg, s_ref[...]).astype(BF16)


def _chdft(u, col_block, cmat, smat, dims):
    t = u.shape[0]
    tf = math.gcd(t, TM_TOKEN)
    return pl.pallas_call(
        _chdft_kernel,
        out_shape=jax.ShapeDtypeStruct((t, 2 * GROUP_W), BF16),
        grid=(t // tf,),
        in_specs=[
            pl.BlockSpec((tf, GROUP_W), lambda i: (i, col_block)),
            pl.BlockSpec(cmat.shape, lambda i: (0, 0)),
            pl.BlockSpec(smat.shape, lambda i: (0, 0)),
        ],
        out_specs=pl.BlockSpec((tf, 2 * GROUP_W), lambda i: (i, 0)),
        compiler_params=_cp(("arbitrary",)),
        name="chdft",
    )(u, cmat, smat)


def _seqdft_kernel(f_ref, z_ref, *rest, scale):
    o_ref, acc = rest[-2], rest[-1]
    k = pl.program_id(2)

    @pl.when(k == 0)
    def _():
        acc[...] = jnp.zeros_like(acc)

    acc[...] += _dot(f_ref[...], z_ref[...])

    @pl.when(k == pl.num_programs(2) - 1)
    def _():
        o_ref[...] = acc[...] * scale


def _seqdft(fmat, z, prev_out, n_seq, seq_len, row_off, total_rows):
    tm = math.gcd(seq_len, TM_DFT)
    tk = math.gcd(seq_len, TK_DFT)
    kh = seq_len // tk
    scale = 1.0 / math.sqrt(seq_len * (GROUP_W // FN_GROUPS))
    in_specs = [
        pl.BlockSpec((tm, tk), lambda b, i, k: (i, k)),
        pl.BlockSpec((tk, GROUP_W), lambda b, i, k: (row_off // tk + b * kh + k % kh, k // kh)),
    ]
    args = [fmat, z]
    aliases = {}
    if prev_out is not None:
        in_specs.append(pl.BlockSpec(memory_space=pl.ANY))
        args.append(prev_out)
        aliases = {2: 0}
    return pl.pallas_call(
        functools.partial(_seqdft_kernel, scale=scale),
        out_shape=jax.ShapeDtypeStruct((total_rows, GROUP_W), F32),
        grid=(n_seq, seq_len // tm, 2 * kh),
        in_specs=in_specs,
        out_specs=pl.BlockSpec((tm, GROUP_W), lambda b, i, k: (row_off // tm + b * (seq_len // tm) + i, 0)),
        scratch_shapes=[pltpu.VMEM((tm, GROUP_W), F32)],
        input_output_aliases=aliases,
        compiler_params=_cp(("arbitrary", "arbitrary", "arbitrary")),
        name="seqdft",
    )(*args)


def _ssd_kernel(*refs, direction, final):
    if final:
        (xbc_ref, dt_ref, a_ref, bias_ref, e_ref, tri_ref, yf_ref, z_ref, dsk_ref, ng_ref,
         o_ref, st_scr) = refs
    else:
        (xbc_ref, dt_ref, a_ref, bias_ref, e_ref, tri_ref, o_ref, st_scr) = refs
    q = SSD_CHUNK
    gw = SSD_HEADS // SSD_GROUPS * SSD_HEAD_DIM
    hpg = SSD_HEADS // SSD_GROUPS
    n = SSD_STATE

    @pl.when(pl.program_id(1) == 0)
    def _():
        st_scr[...] = jnp.zeros_like(st_scr)

    e01 = e_ref[...]
    dtp = _softplus(dt_ref[...] + bias_ref[...])
    a = dtp * a_ref[...]
    acs = _dot_01x(tri_ref[...], a)
    acs_t = acs.T
    acs_e = _dot_x01(acs, e01)
    dt_e = _dot_x01(dtp, e01)
    edge_row = q - 1 if direction == 0 else 0
    edge = acs_e[edge_row:edge_row + 1, :]
    x = xbc_ref[:, :SSD_HEADS * SSD_HEAD_DIM]
    xdt = x * dt_e
    xdt_b = xdt.astype(BF16)
    w_b = (xdt * jnp.exp(edge - acs_e)).astype(BF16)
    e_acs = jnp.exp(acs_e)
    e_edge = jnp.exp(edge)

    li = lax.broadcasted_iota(I32, (q, q), 0)
    si = lax.broadcasted_iota(I32, (q, q), 1)
    causal = (li >= si) if direction == 0 else (li <= si)
    lane = lax.broadcasted_iota(I32, (q, 2 * SSD_HEAD_DIM), 1)
    first_half = lane < SSD_HEAD_DIM

    x_w = SSD_HEADS * SSD_HEAD_DIM
    ys = []
    for g in range(SSD_GROUPS):
        bg = xbc_ref[:, x_w + g * n:x_w + (g + 1) * n].astype(BF16)
        cg = xbc_ref[:, x_w + SSD_GROUPS * n + g * n:x_w + SSD_GROUPS * n + (g + 1) * n].astype(BF16)
        gmat = _dot_nt(cg, bg)
        s_in = st_scr[g]
        y_off = _dot(cg, s_in.astype(BF16)) * e_acs[:, g * gw:(g + 1) * gw]
        y_diag = []
        for pair in range(hpg // 2):
            ms = []
            for r in (2 * pair, 2 * pair + 1):
                ln = direction * SSD_HEADS + g * hpg + r
                seg = acs[:, ln:ln + 1] - acs_t[ln:ln + 1, :]
                ms.append((gmat * jnp.where(causal, jnp.exp(seg), 0.0)).astype(BF16))
            c0 = g * gw + pair * 2 * SSD_HEAD_DIM
            xp = xdt_b[:, c0:c0 + 2 * SSD_HEAD_DIM]
            zero = jnp.zeros_like(xp)
            rhs = jnp.concatenate([jnp.where(first_half, xp, zero), jnp.where(first_half, zero, xp)], axis=0)
            y_diag.append(_dot(jnp.concatenate(ms, axis=1), rhs))
        ys.append(jnp.concatenate(y_diag, axis=1) + y_off)
        st_new = _dot_tn(bg, w_b[:, g * gw:(g + 1) * gw])
        st_scr[g] = e_edge[:, g * gw:(g + 1) * gw] * s_in + st_new
    y = jnp.concatenate(ys, axis=1)
    if final:
        tot = x * dsk_ref[...] + yf_ref[...] + y
        o_ref[...] = _rms(tot * _silu(z_ref[...]), ng_ref[...])
    else:
        o_ref[...] = y


def _ssd_pass(xbc, dt, u_all, z_col_block, a_row, bias_row, e01, tri, yf, dsk, norm_g, dims, direction):
    t = xbc.shape[0]
    b, seq, ctx, nl = dims["B"], dims["L"], dims["C"], dims["NL"]
    q = SSD_CHUNK
    n_c, n_l = ctx // q, seq // q
    final = yf is not None
    width = SSD_HEADS * SSD_HEAD_DIM

    def blk(bi, s):
        c_idx = s if direction == 0 else n_c - 1 - s
        l_idx = (s - n_c) if direction == 0 else n_l - 1 - (s - n_c)
        return jnp.where(s < n_c, nl // q + bi * n_c + c_idx, bi * n_l + l_idx)

    def const(shape):
        return pl.BlockSpec(shape, lambda bi, s: (0,) * len(shape))

    out_spec = pl.BlockSpec((q, width), lambda bi, s: (blk(bi, s), 0))
    in_specs = [
        pl.BlockSpec((q, xbc.shape[1]), lambda bi, s: (blk(bi, s), 0)),
        pl.BlockSpec((q, LANES), lambda bi, s: (blk(bi, s), 0)),
        const((1, LANES)), const((1, LANES)), const(e01.shape), const(tri.shape),
    ]
    args = [xbc, dt, a_row, bias_row, e01, tri]
    if final:
        in_specs += [out_spec, pl.BlockSpec((q, width), lambda bi, s: (blk(bi, s), z_col_block)),
                     const((1, width)), const((1, width))]
        args += [yf, u_all, dsk, norm_g.reshape(1, width)]
    kernel = functools.partial(_ssd_kernel, direction=direction, final=final)
    return pl.pallas_call(
        kernel,
        out_shape=jax.ShapeDtypeStruct((t, width), F32),
        grid=(b, n_c + n_l),
        in_specs=in_specs,
        out_specs=out_spec,
        scratch_shapes=[pltpu.VMEM((SSD_GROUPS, SSD_STATE, width // SSD_GROUPS), F32)],
        compiler_params=_cp(("arbitrary", "arbitrary")),
        name="ssd_dir%d" % direction,
    )(*args)


def _outproj_kernel(att_ref, lru_ref, four_ref, ssd_ref, wo_ref, x_ref, mod_ref, n2_ref, rw_ref, rb_ref,
                    x1_ref, tok_ref, mi_ref, mg_ref, cnt_ref, carry):
    @pl.when(pl.program_id(0) == 0)
    def _():
        carry[...] = jnp.zeros_like(carry)

    acc = _dot(att_ref[...].astype(BF16), wo_ref[0])
    acc += _dot(lru_ref[...].astype(BF16), wo_ref[1])
    acc += _dot(four_ref[...].astype(BF16), wo_ref[2])
    acc += _dot(ssd_ref[...].astype(BF16), wo_ref[3])
    x1 = x_ref[...] + mod_ref[2:3, :] * acc
    x1_ref[...] = x1
    tok = _rms(x1, n2_ref[...]) * (1.0 + mod_ref[4:5, :]) + mod_ref[3:4, :]
    tok_ref[...] = tok

    logits = _dot_hi(tok, rw_ref[...]) + rb_ref[...]
    tm = logits.shape[0]
    lane = lax.broadcasted_iota(I32, logits.shape, 1).astype(F32)
    work = logits
    vals, idxs = [], []
    for _ in range(TOP_K):
        m = jnp.max(work, axis=-1, keepdims=True)
        idx = jnp.min(jnp.where(work == m, lane, float(LANES)), axis=-1, keepdims=True)
        vals.append(m)
        idxs.append(idx)
        work = jnp.where(lane == idx, -jnp.inf, work)
    es = [jnp.exp(v - vals[0]) for v in vals]
    denom = es[0] + es[1] + es[2] + es[3]
    sel = jnp.zeros(logits.shape, F32)
    for idx in idxs:
        sel = sel + (lane == idx).astype(F32)
    ri = lax.broadcasted_iota(I32, (tm, tm), 0)
    ci = lax.broadcasted_iota(I32, (tm, tm), 1)
    strict_lower = (ci < ri).astype(BF16)
    cum = _dot(strict_lower, sel.astype(BF16)) + carry[...]
    carry[...] = carry[...] + jnp.sum(sel, axis=0, keepdims=True)
    cnt_ref[...] = jnp.broadcast_to(carry[...], cnt_ref.shape)
    mi = jnp.zeros(logits.shape, F32)
    mg = jnp.zeros(logits.shape, F32)
    for k in range(TOP_K):
        rank = jnp.sum(jnp.where(lane == idxs[k], cum, 0.0), axis=-1, keepdims=True)
        mi = jnp.where(lane == float(k), idxs[k], mi)
        mi = jnp.where(lane == float(TOP_K + k), rank, mi)
        mg = jnp.where(lane == float(k), es[k] / denom, mg)
    mi_ref[...] = mi.astype(I32)
    mg_ref[...] = mg


def _outproj(att, lru, four, ssd, wo4, x, mod6, norm2, rw, rb, dims):
    t, d = x.shape
    tm = math.gcd(math.gcd(dims["L"], dims["TC"]), TM_TOKEN)
    mod_row = dims["mod_row"]
    feat = pl.BlockSpec((tm, GROUP_W), lambda i: (i, 0))
    row_d = pl.BlockSpec((tm, d), lambda i: (i, 0))
    row_l = pl.BlockSpec((tm, LANES), lambda i: (i, 0))
    return pl.pallas_call(
        _outproj_kernel,
        out_shape=(jax.ShapeDtypeStruct((t, d), F32), jax.ShapeDtypeStruct((t, d), F32),
                   jax.ShapeDtypeStruct((t, LANES), I32), jax.ShapeDtypeStruct((t, LANES), F32),
                   jax.ShapeDtypeStruct((SUBLANES, LANES), F32)),
        grid=(t // tm,),
        in_specs=[feat, feat, feat, feat,
                  pl.BlockSpec(wo4.shape, lambda i: (0, 0, 0)),
                  row_d,
                  pl.BlockSpec((None, 6, d), lambda i: (mod_row(i * tm), 0, 0)),
                  pl.BlockSpec((1, d), lambda i: (0, 0)),
                  pl.BlockSpec((d, LANES), lambda i: (0, 0)),
                  pl.BlockSpec((1, LANES), lambda i: (0, 0))],
        out_specs=(row_d, row_d, row_l, row_l, pl.BlockSpec((SUBLANES, LANES), lambda i: (0, 0))),
        scratch_shapes=[pltpu.VMEM((1, LANES), F32)],
        compiler_params=_cp(("arbitrary",)),
        name="outproj_router",
    )(att, lru, four, ssd, wo4, x, mod6, norm2.reshape(1, d), rw, rb)


def _dispatch_kernel(dest_ref, tok_ref, xs_in_ref, xs_ref, sem):
    del xs_in_ref
    td = tok_ref.shape[0]

    def issue(r, c):
        for k in range(TOP_K):
            pltpu.make_async_copy(tok_ref.at[pl.ds(r, 1)], xs_ref.at[pl.ds(dest_ref[r * TOP_K + k], 1)],
                                  sem).start()
        return c

    lax.fori_loop(0, td, issue, 0)

    def drain(r, c):
        for k in range(TOP_K):
            pltpu.make_async_copy(tok_ref.at[pl.ds(r, 1)], xs_ref.at[pl.ds(0, 1)], sem).wait()
        return c

    lax.fori_loop(0, td, drain, 0)


def _dispatch(tok, dest_flat, xs_zero):
    t, d = tok.shape
    td = math.gcd(t, TM_TOKEN)
    return pl.pallas_call(
        _dispatch_kernel,
        out_shape=jax.ShapeDtypeStruct(xs_zero.shape, F32),
        grid=(t // td,),
        in_specs=[
            pl.BlockSpec((td * TOP_K,), lambda i: (i,), memory_space=pltpu.SMEM),
            pl.BlockSpec((td, d), lambda i: (i, 0)),
            pl.BlockSpec(memory_space=pl.ANY),
        ],
        out_specs=pl.BlockSpec(memory_space=pl.ANY),
        scratch_shapes=[pltpu.SemaphoreType.DMA(())],
        input_output_aliases={2: 0},
        compiler_params=_cp(("arbitrary",)),
        name="moe_dispatch",
    )(dest_flat, tok, xs_zero)


def _expert_kernel(be_ref, nu_ref, xs_ref, wg_ref, wu_ref, bg_ref, bu_ref, wd_ref, bd_ref, ys_ref):
    del be_ref
    i = pl.program_id(0)

    @pl.when(i < nu_ref[0])
    def _():
        x = xs_ref[...].astype(BF16)
        g = _dot(x, wg_ref[...]) + bg_ref[...]
        u = _dot(x, wu_ref[...]) + bu_ref[...]
        gate = jnp.minimum(g, SWIGLU_LIMIT)
        up = jnp.clip(u, -SWIGLU_LIMIT, SWIGLU_LIMIT)
        act = gate * _sigmoid(SWIGLU_ALPHA * gate) * (up + 1.0)
        ys_ref[...] = _dot(act.astype(BF16), wd_ref[...]) + bd_ref[...]

    @pl.when(i >= nu_ref[0])
    def _():
        ys_ref[...] = jnp.zeros_like(ys_ref)


def _experts(xs, blk_e, n_used, wg, wu, bg, bu, wd, bd):
    p, d = xs.shape
    bm = BM_MOE
    ff = wg.shape[2]

    def xrow(i, be, nu):
        return (jnp.minimum(i, nu[0] - 1), 0)

    def wsel(i, be, nu):
        return (be[i], 0, 0)

    grid_spec = pltpu.PrefetchScalarGridSpec(
        num_scalar_prefetch=2,
        grid=(p // bm,),
        in_specs=[
            pl.BlockSpec((bm, d), xrow),
            pl.BlockSpec((None, d, ff), wsel),
            pl.BlockSpec((None, d, ff), wsel),
            pl.BlockSpec((None, 1, ff), wsel),
            pl.BlockSpec((None, 1, ff), wsel),
            pl.BlockSpec((None, ff, d), wsel),
            pl.BlockSpec((None, 1, d), wsel),
        ],
        out_specs=pl.BlockSpec((bm, d), lambda i, be, nu: (i, 0)),
    )
    return pl.pallas_call(
        _expert_kernel,
        out_shape=jax.ShapeDtypeStruct((p, d), F32),
        grid_spec=grid_spec,
        compiler_params=_cp(("arbitrary",)),
        name="moe_experts",
    )(blk_e, n_used, xs, wg, wu, bg, bu, wd, bd)


def _combine_kernel(dest_ref, ys_ref, x1_ref, mg_ref, mod_ref, o_ref, buf, sem):
    tm = x1_ref.shape[0]

    def issue(r, c):
        for k in range(TOP_K):
            pltpu.make_async_copy(ys_ref.at[pl.ds(dest_ref[r * TOP_K + k], 1)], buf.at[k, pl.ds(r, 1)],
                                  sem).start()
        return c

    lax.fori_loop(0, tm, issue, 0)

    def drain(r, c):
        for k in range(TOP_K):
            pltpu.make_async_copy(ys_ref.at[pl.ds(0, 1)], buf.at[k, pl.ds(r, 1)], sem).wait()
        return c

    lax.fori_loop(0, tm, drain, 0)
    mg = mg_ref[...]
    ffn = mg[:, 0:1] * buf[0]
    for k in range(1, TOP_K):
        ffn = ffn + mg[:, k:k + 1] * buf[k]
    o_ref[...] = x1_ref[...] + mod_ref[5:6, :] * ffn


def _combine(ys, dest_flat, x1, mg, mod6, dims):
    t, d = x1.shape
    tm = math.gcd(math.gcd(dims["L"], dims["TC"]), TM_COMBINE)
    mod_row = dims["mod_row"]
    return pl.pallas_call(
        _combine_kernel,
        out_shape=jax.ShapeDtypeStruct((t, d), F32),
        grid=(t // tm,),
        in_specs=[
            pl.BlockSpec((tm * TOP_K,), lambda i: (i,), memory_space=pltpu.SMEM),
            pl.BlockSpec(memory_space=pl.ANY),
            pl.BlockSpec((tm, d), lambda i: (i, 0)),
            pl.BlockSpec((tm, LANES), lambda i: (i, 0)),
            pl.BlockSpec((None, 6, d), lambda i: (mod_row(i * tm), 0, 0)),
        ],
        out_specs=pl.BlockSpec((tm, d), lambda i: (i, 0)),
        scratch_shapes=[pltpu.VMEM((TOP_K, tm, d), F32), pltpu.SemaphoreType.DMA(())],
        compiler_params=_cp(("arbitrary",)),
        name="moe_combine",
    )(dest_flat, ys, x1, mg, mod6)


def _rope_tables(seq, ctx):
    rows = seq // GRID_W
    row = np.repeat(np.arange(rows), GRID_W).astype(np.float32)
    col = np.tile(np.arange(GRID_W), rows).astype(np.float32)
    pairs = HEAD_DIM // 4
    inv = (ROPE_THETA ** (-np.arange(pairs, dtype=np.float32) / pairs)).astype(np.float32)
    ang = np.concatenate([row[:, None] * inv, col[:, None] * inv], axis=-1)
    ang = np.concatenate([ang, ang], axis=-1).astype(np.float32)
    sign = np.where(np.arange(HEAD_DIM) < HEAD_DIM // 2, -1.0, 1.0).astype(np.float32)
    cos = np.concatenate([np.ones((ctx, HEAD_DIM), np.float32), np.cos(ang)], axis=0)
    sin = np.concatenate([np.zeros((ctx, HEAD_DIM), np.float32), np.sin(ang) * sign], axis=0)
    return jnp.asarray(cos), jnp.asarray(sin)


def _dft_cos_sin(n):
    k = np.arange(n, dtype=np.int64)
    ang = 2.0 * np.pi * ((k[:, None] * k[None, :]) % n).astype(np.float64) / n
    return np.cos(ang), np.sin(ang)


def _seq_dft_matrix(n):
    c, s = _dft_cos_sin(n)
    return jnp.asarray(np.concatenate([c, -s], axis=1), dtype=BF16)


def _ssd_constants(direction):
    e = np.zeros((LANES, SSD_HEADS * SSD_HEAD_DIM), np.float32)
    for h in range(SSD_HEADS):
        e[direction * SSD_HEADS + h, h * SSD_HEAD_DIM:(h + 1) * SSD_HEAD_DIM] = 1.0
    li = np.arange(SSD_CHUNK)
    tri = (li[:, None] >= li[None, :]) if direction == 0 else (li[:, None] <= li[None, :])
    return jnp.asarray(e, dtype=BF16), jnp.asarray(tri.astype(np.float32), dtype=BF16)


def _lane_row(v):
    flat = v.reshape(1, -1).astype(F32)
    return jnp.pad(flat, ((0, 0), (0, LANES - flat.shape[1])))


def kernel(x, c, ctx, c_ctx, w_mod, b_mod, norm1, norm2, w_in, w_out, q_norm, k_norm, lru_conv_w, lru_conv_b,
           lru_wa, lru_ba, lru_wx, lru_bx, lru_lam, ssd_conv_w, ssd_conv_b, ssd_a_log, ssd_dt_bias, ssd_d,
           ssd_norm, router_w, router_b, exp_w_gu, exp_b_gu, exp_w_dn, exp_b_dn):
    b, seq, d = x.shape
    n_ctx = ctx.shape[1]
    depth = w_in.shape[0]
    n_exp = router_w.shape[2]
    ff = exp_w_dn.shape[2]
    nl, tc_rows = b * seq, b * n_ctx
    t = nl + tc_rows
    assert seq % SSD_CHUNK == 0 and n_ctx % SSD_CHUNK == 0 and seq % n_ctx == 0 and nl % n_ctx == 0
    assert seq % GRID_W == 0 and n_exp <= LANES and b + 1 <= 2 * SUBLANES

    def mod_row(r):
        return jnp.where(r < nl, r // seq, b)

    dims = dict(B=b, L=seq, C=n_ctx, NL=nl, TC=tc_rows, mod_row=mod_row)

    xa = jnp.concatenate([x.reshape(nl, d), ctx.reshape(tc_rows, d)], axis=0)
    cc = jnp.concatenate([c, c_ctx[None, :], jnp.zeros((2 * SUBLANES - b - 1, d), F32)], axis=0)
    mods = _mod_all(cc, w_mod, b_mod).reshape(depth, 2 * SUBLANES, 6, d)

    cos_t, sin_t = _rope_tables(seq, n_ctx)
    c128, s128 = _dft_cos_sin(GROUP_W // FN_GROUPS)
    c128, s128 = jnp.asarray(c128, dtype=BF16), jnp.asarray(s128, dtype=BF16)
    f_lat, f_ctx = _seq_dft_matrix(seq), _seq_dft_matrix(n_ctx)
    ssd_const = [_ssd_constants(0), _ssd_constants(1)]

    n_main = w_in.shape[2] - 2 * SSD_HEADS
    col_lx, col_lg, col_f, col_z, col_xbc = 1024, 1536, 2048, 2560, 3072
    xbc_w = n_main - col_xbc

    p_rows = t * TOP_K + n_exp * BM_MOE
    n_blk = p_rows // BM_MOE

    for l in range(depth):
        w_main = w_in[l, :, :n_main].astype(BF16)
        w_dt = jnp.pad(w_in[l, :, n_main:], ((0, 0), (0, LANES - 2 * SSD_HEADS))).astype(BF16)
        mod6 = mods[l]
        u, dt = _inproj(xa, mod6, norm1[l], w_main, w_dt, dims)

        qn, kn, vb = _qkprep(u, cos_t, sin_t, q_norm[l], k_norm[l], dims)
        att = _attention(qn, kn, vb, dims)

        lxc = _dwconv(u, col_lx, GROUP_W, lru_conv_w[l], lru_conv_b[l], dims, act=False)
        wa, wx = lru_wa[l].astype(BF16), lru_wx[l].astype(BF16)
        hb = _lru_pass(lxc, u, None, wa[1], lru_ba[l, 1], wx[1], lru_bx[l, 1], lru_lam[l, 1], None, dims, 1)
        lru = _lru_pass(lxc, u, col_lg // GROUP_W, wa[0], lru_ba[l, 0], wx[0], lru_bx[l, 0], lru_lam[l, 0],
                        hb, dims, 0)

        z = _chdft(u, col_f // GROUP_W, c128, s128, dims)
        four = _seqdft(f_lat, z, jnp.zeros((t, GROUP_W), F32), b, seq, 0, t)
        four = _seqdft(f_ctx, z, four, b, n_ctx, nl, t)

        xbc = _dwconv(u, col_xbc, xbc_w, ssd_conv_w[l], ssd_conv_b[l], dims, act=True)
        a_row = _lane_row(-jnp.exp(ssd_a_log[l]))
        bias_row = _lane_row(ssd_dt_bias[l])
        dsk = jnp.repeat(ssd_d[l], SSD_HEAD_DIM).reshape(1, -1)
        yf = _ssd_pass(xbc, dt, u, None, a_row, bias_row, *ssd_const[0], None, None, None, dims, 0)
        ssd = _ssd_pass(xbc, dt, u, col_z // GROUP_W, a_row, bias_row, *ssd_const[1], yf, dsk, ssd_norm[l],
                        dims, 1)

        wo4 = w_out[l].astype(BF16).reshape(4, GROUP_W, d)
        rw = jnp.pad(router_w[l], ((0, 0), (0, LANES - n_exp)))
        rb = jnp.pad(router_b[l], (0, LANES - n_exp), constant_values=NEG_BIG).reshape(1, LANES)
        x1, tok, mi, mg, cnt = _outproj(att, lru, four, ssd, wo4, xa, mod6, norm2[l], rw, rb, dims)

        counts = cnt[0, :n_exp].astype(I32)
        pcounts = (counts + BM_MOE - 1) // BM_MOE * BM_MOE
        pends = jnp.cumsum(pcounts)
        pstarts = pends - pcounts
        dest = (pstarts[mi[:, :TOP_K]] + mi[:, TOP_K:2 * TOP_K]).reshape(-1)
        blk_e = jnp.minimum(jnp.searchsorted(pends, jnp.arange(n_blk, dtype=I32) * BM_MOE, side="right"),
                            n_exp - 1).astype(I32)
        n_used = (pends[-1:] // BM_MOE).astype(I32)

        xs = _dispatch(tok, dest, jnp.zeros((p_rows, d), F32))
        gu = exp_w_gu[l].astype(BF16).reshape(n_exp, d, ff, 2)
        bgu = exp_b_gu[l].reshape(n_exp, 1, ff, 2)
        ys = _experts(xs, blk_e, n_used, gu[..., 0], gu[..., 1], bgu[..., 0], bgu[..., 1],
                      exp_w_dn[l].astype(BF16), exp_b_dn[l].reshape(n_exp, 1, d))
        xa = _combine(ys, dest, x1, mg, mod6, dims)

    return xa[:nl].reshape(b, seq, d)
```

```python
import functools
import math

import jax
import jax.numpy as jnp
import numpy as np
from jax import lax
from jax.experimental import pallas as pl
from jax.experimental.pallas import tpu as pltpu

F32 = jnp.float32
BF16 = jnp.bfloat16
I32 = jnp.int32

HEAD_DIM = 128
ATT_Q_HEADS = 4
ATT_KV_HEADS = 2
GRID_W = 64
ROPE_THETA = 10000.0
CONV_W = 4
LRU_BLOCKS = 4
LRU_C = 8.0
FN_GROUPS = 4
SSD_HEAD_DIM = 64
SSD_HEADS = 8
SSD_GROUPS = 2
SSD_STATE = 128
SSD_CHUNK = 128
TOP_K = 4
SWIGLU_LIMIT = 7.0
SWIGLU_ALPHA = 1.702
EPS = 1e-6
GROUP_W = 512

LANES = 128
SUBLANES = 8
VMEM_LIMIT = 56 * 1024 * 1024

NEG_BIG = -1e30

TM_INPROJ = 1024
TN_INPROJ = 512
TM_TOKEN = 256
TQ_ATT = 256
TK_ATT = 512
TC_SCAN = 256
TM_DFT = 1024
TK_DFT = 1024
BM_MOE = 512
TM_COMBINE = 128


def _cp(sem, vmem=VMEM_LIMIT):
    return pltpu.CompilerParams(dimension_semantics=sem, vmem_limit_bytes=vmem)


def _dot(a, b):
    return jnp.dot(a, b, preferred_element_type=F32)


def _dot_nt(a, b):
    return lax.dot_general(a, b, (((1,), (1,)), ((), ())), preferred_element_type=F32)


def _dot_tn(a, b):
    return lax.dot_general(a, b, (((0,), (0,)), ((), ())), preferred_element_type=F32)


def _split3(x):
    hi = x.astype(BF16)
    r = x - hi.astype(F32)
    mid = r.astype(BF16)
    lo = (r - mid.astype(F32)).astype(BF16)
    return hi, mid, lo


def _dot_x01(x, m01):
    hi, mid, lo = _split3(x)
    return _dot(hi, m01) + _dot(mid, m01) + _dot(lo, m01)


def _dot_01x(m01, x):
    hi, mid, lo = _split3(x)
    return _dot(m01, hi) + _dot(m01, mid) + _dot(m01, lo)


def _dot_hi(a, b):
    ah = a.astype(BF16)
    al = (a - ah.astype(F32)).astype(BF16)
    bh = b.astype(BF16)
    bl = (b - bh.astype(F32)).astype(BF16)
    return _dot(ah, bh) + _dot(ah, bl) + _dot(al, bh)


def _sigmoid(x):
    return 1.0 / (1.0 + jnp.exp(-x))


def _silu(x):
    return x * _sigmoid(x)


def _softplus(x):
    return jnp.maximum(x, 0.0) + jnp.log(1.0 + jnp.exp(-jnp.abs(x)))


def _gelu_tanh(x):
    c = math.sqrt(2.0 / math.pi)
    return 0.5 * x * (1.0 + jnp.tanh(c * (x + 0.044715 * (x * x * x))))


def _rms(x, g):
    return x * lax.rsqrt(jnp.mean(x * x, axis=-1, keepdims=True) + EPS) * g


def _mod_kernel(cc_ref, w_ref, b_ref, o_ref):
    s = _silu(cc_ref[...])
    o_ref[...] = _dot_hi(s, w_ref[...]) + b_ref[...]


def _mod_all(cc, w_mod, b_mod):
    n_layers, d, d6 = w_mod.shape
    tn = 1024
    rows = cc.shape[0]
    return pl.pallas_call(
        _mod_kernel,
        out_shape=jax.ShapeDtypeStruct((n_layers, rows, d6), F32),
        grid=(n_layers, d6 // tn),
        in_specs=[
            pl.BlockSpec((rows, d), lambda l, j: (0, 0)),
            pl.BlockSpec((None, d, tn), lambda l, j: (l, 0, j)),
            pl.BlockSpec((None, 1, tn), lambda l, j: (l, 0, j)),
        ],
        out_specs=pl.BlockSpec((None, rows, tn), lambda l, j: (l, 0, j)),
        compiler_params=_cp(("arbitrary", "arbitrary")),
        name="mod_all",
    )(cc, w_mod, b_mod.reshape(n_layers, 1, d6))


def _inproj_kernel(x_ref, mod_ref, n1_ref, w_ref, wdt_ref, u_ref, dt_ref, h_scr):
    @pl.when(pl.program_id(1) == 0)
    def _():
        x = x_ref[...]
        h = _rms(x, n1_ref[...]) * (1.0 + mod_ref[1:2, :]) + mod_ref[0:1, :]
        hb = h.astype(BF16)
        h_scr[...] = hb
        dt_ref[...] = _dot(hb, wdt_ref[...])

    u_ref[...] = _dot(h_scr[...], w_ref[...])


def _inproj(x, mod6, norm1, w_main, w_dt, dims):
    t, d = x.shape
    n_main = w_main.shape[1]
    tm = math.gcd(math.gcd(dims["L"], dims["TC"]), TM_INPROJ)
    tn = TN_INPROJ
    mod_row = dims["mod_row"]
    return pl.pallas_call(
        _inproj_kernel,
        out_shape=(jax.ShapeDtypeStruct((t, n_main), F32), jax.ShapeDtypeStruct((t, LANES), F32)),
        grid=(t // tm, n_main // tn),
        in_specs=[
            pl.BlockSpec((tm, d), lambda i, j: (i, 0)),
            pl.BlockSpec((None, 6, d), lambda i, j: (mod_row(i * tm), 0, 0)),
            pl.BlockSpec((1, d), lambda i, j: (0, 0)),
            pl.BlockSpec((d, tn), lambda i, j: (0, j)),
            pl.BlockSpec((d, LANES), lambda i, j: (0, 0)),
        ],
        out_specs=(pl.BlockSpec((tm, tn), lambda i, j: (i, j)),
                   pl.BlockSpec((tm, LANES), lambda i, j: (i, 0))),
        scratch_shapes=[pltpu.VMEM((tm, d), BF16)],
        compiler_params=_cp(("arbitrary", "arbitrary")),
        name="inproj",
    )(x, mod6, norm1.reshape(1, d), w_main, w_dt)


def _qkprep_kernel(q_ref, k_ref, v_ref, cos_ref, sin_ref, qn_ref, kn_ref, qo_ref, ko_ref, vo_ref):
    cos = cos_ref[...]
    sin = sin_ref[...]

    def prep(xh, g):
        y = _rms(xh, g)
        return y * cos + pltpu.roll(y, HEAD_DIM // 2, 1) * sin

    for h in range(ATT_Q_HEADS):
        sl = slice(h * HEAD_DIM, (h + 1) * HEAD_DIM)
        qo_ref[:, sl] = prep(q_ref[:, sl], qn_ref[...]).astype(BF16)
    for h in range(ATT_KV_HEADS):
        sl = slice(h * HEAD_DIM, (h + 1) * HEAD_DIM)
        ko_ref[:, sl] = prep(k_ref[:, sl], kn_ref[...]).astype(BF16)
    ones = jnp.ones((v_ref.shape[0], HEAD_DIM), BF16)
    for h in range(ATT_KV_HEADS):
        vo_ref[:, 2 * h * HEAD_DIM:(2 * h + 1) * HEAD_DIM] = v_ref[:, h * HEAD_DIM:(h + 1) * HEAD_DIM].astype(BF16)
        vo_ref[:, (2 * h + 1) * HEAD_DIM:(2 * h + 2) * HEAD_DIM] = ones


def _qkprep(u, cos_t, sin_t, q_norm, k_norm, dims):
    t = u.shape[0]
    tp = math.gcd(math.gcd(dims["L"], dims["C"]), TM_TOKEN)
    nl, seq, ctx = dims["NL"], dims["L"], dims["C"]

    def tab(i):
        r = i * tp
        return jnp.where(r < nl, ctx // tp + (r % seq) // tp, 0)

    qw, kw = ATT_Q_HEADS * HEAD_DIM, ATT_KV_HEADS * HEAD_DIM
    return pl.pallas_call(
        _qkprep_kernel,
        out_shape=(jax.ShapeDtypeStruct((t, qw), BF16), jax.ShapeDtypeStruct((t, kw), BF16),
                   jax.ShapeDtypeStruct((t, 2 * kw), BF16)),
        grid=(t // tp,),
        in_specs=[
            pl.BlockSpec((tp, qw), lambda i: (i, 0)),
            pl.BlockSpec((tp, kw), lambda i: (i, qw // kw)),
            pl.BlockSpec((tp, kw), lambda i: (i, qw // kw + 1)),
            pl.BlockSpec((tp, HEAD_DIM), lambda i: (tab(i), 0)),
            pl.BlockSpec((tp, HEAD_DIM), lambda i: (tab(i), 0)),
            pl.BlockSpec((1, HEAD_DIM), lambda i: (0, 0)),
            pl.BlockSpec((1, HEAD_DIM), lambda i: (0, 0)),
        ],
        out_specs=(pl.BlockSpec((tp, qw), lambda i: (i, 0)),
                   pl.BlockSpec((tp, kw), lambda i: (i, 0)),
                   pl.BlockSpec((tp, 2 * kw), lambda i: (i, 0))),
        compiler_params=_cp(("arbitrary",)),
        name="qkprep",
    )(u, u, u, cos_t, sin_t, q_norm.reshape(1, HEAD_DIM), k_norm.reshape(1, HEAD_DIM))


def _attn_kernel(q_ref, kc_ref, vc_ref, kl_ref, vl_ref, o_ref, *, n_lat_q, n_lat_chunks, tk):
    tq = q_ref.shape[0]
    c2 = HEAD_DIM ** -0.5 * math.log2(math.e)
    q2 = jnp.concatenate([q_ref[:, :HEAD_DIM], q_ref[:, HEAD_DIM:]], axis=0)

    def step(k, v1, carry):
        m, acc = carry
        s = _dot_nt(q2, k) * c2
        m_new = jnp.maximum(m, jnp.max(s, axis=-1, keepdims=True))
        alpha = jnp.exp2(m - m_new)
        p = jnp.exp2((s - m_new).astype(BF16))
        acc = alpha * acc + _dot(p, v1)
        return m_new, acc

    init = (jnp.full((2 * tq, 1), NEG_BIG, F32), jnp.zeros((2 * tq, 2 * HEAD_DIM), F32))
    carry = step(kc_ref[...], vc_ref[...], init)

    def body(j, c):
        off = pl.multiple_of(j * tk, tk)
        return step(kl_ref[pl.ds(off, tk), :], vl_ref[pl.ds(off, tk), :], c)

    n = jnp.where(pl.program_id(2) < n_lat_q, n_lat_chunks, 0)
    m, acc = lax.fori_loop(0, n, body, carry)
    o = acc[:, :HEAD_DIM] / acc[:, HEAD_DIM:]
    o_ref[:, :HEAD_DIM] = o[:tq]
    o_ref[:, HEAD_DIM:] = o[tq:]


def _attention(qn, kn, vb, dims):
    t = qn.shape[0]
    b, seq, ctx, nl = dims["B"], dims["L"], dims["C"], dims["NL"]
    tq = math.gcd(math.gcd(seq, ctx), TQ_ATT)
    tk = math.gcd(seq, TK_ATT)
    n_lat_q, n_ctx_q = seq // tq, ctx // tq
    rep = ATT_Q_HEADS // ATT_KV_HEADS

    def qrow(bi, i):
        return jnp.where(i < n_lat_q, bi * n_lat_q + i, nl // tq + bi * n_ctx_q + (i - n_lat_q))

    kernel = functools.partial(_attn_kernel, n_lat_q=n_lat_q, n_lat_chunks=seq // tk, tk=tk)
    return pl.pallas_call(
        kernel,
        out_shape=jax.ShapeDtypeStruct((t, ATT_Q_HEADS * HEAD_DIM), F32),
        grid=(b, ATT_KV_HEADS, n_lat_q + n_ctx_q),
        in_specs=[
            pl.BlockSpec((tq, rep * HEAD_DIM), lambda bi, g, i: (qrow(bi, i), g)),
            pl.BlockSpec((ctx, HEAD_DIM), lambda bi, g, i: (nl // ctx + bi, g)),
            pl.BlockSpec((ctx, 2 * HEAD_DIM), lambda bi, g, i: (nl // ctx + bi, g)),
            pl.BlockSpec((seq, HEAD_DIM), lambda bi, g, i: (bi, g)),
            pl.BlockSpec((seq, 2 * HEAD_DIM), lambda bi, g, i: (bi, g)),
        ],
        out_specs=pl.BlockSpec((tq, rep * HEAD_DIM), lambda bi, g, i: (qrow(bi, i), g)),
        compiler_params=_cp(("arbitrary", "arbitrary", "arbitrary")),
        name="attention",
    )(qn, kn, vb, kn, vb)


def _dwconv_kernel(x_ref, p_ref, n_ref, w_ref, b_ref, o_ref, *, rows_lat, seq, ctx, act):
    tc = x_ref.shape[0]
    r0 = pl.program_id(0) * tc
    pos = jnp.where(r0 < rows_lat, r0 % seq, (r0 - rows_lat) % ctx)
    length = jnp.where(r0 < rows_lat, seq, ctx)
    has_prev = (pos > 0).astype(F32)
    has_next = (pos + tc < length).astype(F32)
    x = x_ref[...]
    row = lax.broadcasted_iota(I32, x.shape, 0)
    p1 = p_ref[SUBLANES - 1:SUBLANES, :] * has_prev
    p2 = p_ref[SUBLANES - 2:SUBLANES - 1, :] * has_prev
    n0 = n_ref[0:1, :] * has_next
    x1 = jnp.where(row == 0, p1, pltpu.roll(x, 1, 0))
    x2 = jnp.where(row == 0, p2, jnp.where(row == 1, p1, pltpu.roll(x, 2, 0)))
    xn = jnp.where(row == tc - 1, n0, pltpu.roll(x, tc - 1, 0))
    y = w_ref[0:1, :] * x2 + w_ref[1:2, :] * x1 + w_ref[2:3, :] * x + w_ref[3:4, :] * xn + b_ref[...]
    o_ref[...] = _silu(y) if act else y


def _dwconv(u, col_off, width, w, bias, dims, act):
    t = u.shape[0]
    tc = math.gcd(math.gcd(dims["L"], dims["C"]), TM_TOKEN)
    cb = col_off // width
    per = tc // SUBLANES
    last8 = t // SUBLANES - 1
    kernel = functools.partial(_dwconv_kernel, rows_lat=dims["NL"], seq=dims["L"], ctx=dims["C"], act=act)
    return pl.pallas_call(
        kernel,
        out_shape=jax.ShapeDtypeStruct((t, width), F32),
        grid=(t // tc,),
        in_specs=[
            pl.BlockSpec((tc, width), lambda i: (i, cb)),
            pl.BlockSpec((SUBLANES, width), lambda i: (jnp.maximum(i * per - 1, 0), cb)),
            pl.BlockSpec((SUBLANES, width), lambda i: (jnp.minimum((i + 1) * per, last8), cb)),
            pl.BlockSpec((CONV_W, width), lambda i: (0, 0)),
            pl.BlockSpec((1, width), lambda i: (0, 0)),
        ],
        out_specs=pl.BlockSpec((tc, width), lambda i: (i, 0)),
        compiler_params=_cp(("arbitrary",)),
        name="dwconv",
    )(u, u, u, w, bias.reshape(1, width))


def _lru_kernel(*refs, direction, final):
    if final:
        (u_ref, wa_ref, ba_ref, wx_ref, bx_ref, lam_ref, hb_ref, lg_ref, o_ref, h0_scr) = refs
    else:
        (u_ref, wa_ref, ba_ref, wx_ref, bx_ref, lam_ref, o_ref, h0_scr) = refs
    tc, width = u_ref.shape
    bw = width // LRU_BLOCKS

    @pl.when(pl.program_id(1) == 0)
    def _():
        h0_scr[...] = jnp.zeros_like(h0_scr)

    u = u_ref[...]
    ub = u.astype(BF16)
    ga = jnp.concatenate([_dot(ub[:, n * bw:(n + 1) * bw], wa_ref[n]) for n in range(LRU_BLOCKS)], axis=1)
    gx = jnp.concatenate([_dot(ub[:, n * bw:(n + 1) * bw], wx_ref[n]) for n in range(LRU_BLOCKS)], axis=1)
    r = _sigmoid(ga + ba_ref[...])
    i = _sigmoid(gx + bx_ref[...])
    log_a = (-LRU_C) * r * _softplus(-lam_ref[...])
    a = jnp.exp(log_a)
    bv = jnp.sqrt(1.0 - a * a) * (i * u)

    row = lax.broadcasted_iota(I32, a.shape, 0)
    s = 1
    while s < tc:
        if direction == 0:
            keep = row >= s
            a_s = jnp.where(keep, pltpu.roll(a, s, 0), 1.0)
            b_s = jnp.where(keep, pltpu.roll(bv, s, 0), 0.0)
        else:
            keep = row < tc - s
            a_s = jnp.where(keep, pltpu.roll(a, tc - s, 0), 1.0)
            b_s = jnp.where(keep, pltpu.roll(bv, tc - s, 0), 0.0)
        bv = a * b_s + bv
        a = a * a_s
        s *= 2
    h = bv + a * h0_scr[...]
    h0_scr[...] = h[tc - 1:tc, :] if direction == 0 else h[0:1, :]
    if final:
        o_ref[...] = (h + hb_ref[...]) * _gelu_tanh(lg_ref[...])
    else:
        o_ref[...] = h


def _lru_pass(uconv, u_all, lg_col_block, wa, ba, wx, bx, lam, hb, dims, direction):
    t, width = uconv.shape
    b, seq, ctx, nl = dims["B"], dims["L"], dims["C"], dims["NL"]
    tc = math.gcd(math.gcd(seq, ctx), TC_SCAN)
    n_c, n_l = ctx // tc, seq // tc
    final = hb is not None
    bw = width // LRU_BLOCKS

    def blk(bi, s):
        c_idx = s if direction == 0 else n_c - 1 - s
        l_idx = (s - n_c) if direction == 0 else n_l - 1 - (s - n_c)
        return jnp.where(s < n_c, nl // tc + bi * n_c + c_idx, bi * n_l + l_idx)

    row_spec = pl.BlockSpec((tc, width), lambda bi, s: (blk(bi, s), 0))
    vec = pl.BlockSpec((1, width), lambda bi, s: (0, 0))
    wspec = pl.BlockSpec((LRU_BLOCKS, bw, bw), lambda bi, s: (0, 0, 0))
    in_specs = [row_spec, wspec, vec, wspec, vec, vec]
    args = [uconv, wa, ba.reshape(1, width), wx, bx.reshape(1, width), lam.reshape(1, width)]
    if final:
        in_specs += [row_spec, pl.BlockSpec((tc, width), lambda bi, s: (blk(bi, s), lg_col_block))]
        args += [hb, u_all]
    kernel = functools.partial(_lru_kernel, direction=direction, final=final)
    return pl.pallas_call(
        kernel,
        out_shape=jax.ShapeDtypeStruct((t, width), F32),
        grid=(b, n_c + n_l),
        in_specs=in_specs,
        out_specs=row_spec,
        scratch_shapes=[pltpu.VMEM((1, width), F32)],
        compiler_params=_cp(("arbitrary", "arbitrary")),
        name="lru_dir%d" % direction,
    )(*args)


def _chdft_kernel(x_ref, c_ref, s_ref, z_ref):
    gw = c_ref.shape[0]
    n_g = x_ref.shape[1] // gw
    for g in range(n_g):
        xg = x_ref[:, g * gw:(g + 1) * gw].astype(BF16)
        z_ref[:, g * gw:(g + 1) * gw] = _dot(xg, c_ref[...]).astype(BF16)
        z_ref[:, (n_g + g) * gw:(n_g + g + 1) * gw] = _dot(xg, s_ref[...]).astype(BF16)


def _chdft(u, col_block, cmat, smat, dims):
    t = u.shape[0]
    tf = math.gcd(t, TM_TOKEN)
    return pl.pallas_call(
        _chdft_kernel,
        out_shape=jax.ShapeDtypeStruct((t, 2 * GROUP_W), BF16),
        grid=(t // tf,),
        in_specs=[
            pl.BlockSpec((tf, GROUP_W), lambda i: (i, col_block)),
            pl.BlockSpec(cmat.shape, lambda i: (0, 0)),
            pl.BlockSpec(smat.shape, lambda i: (0, 0)),
        ],
        out_specs=pl.BlockSpec((tf, 2 * GROUP_W), lambda i: (i, 0)),
        compiler_params=_cp(("arbitrary",)),
        name="chdft",
    )(u, cmat, smat)


def _seqdft_kernel(f_ref, z_ref, *rest, scale):
    o_ref, acc = rest[-2], rest[-1]
    k = pl.program_id(2)

    @pl.when(k == 0)
    def _():
        acc[...] = jnp.zeros_like(acc)

    acc[...] += _dot(f_ref[...], z_ref[...])

    @pl.when(k == pl.num_programs(2) - 1)
    def _():
        o_ref[...] = acc[...] * scale


def _seqdft(fmat, z, prev_out, n_seq, seq_len, row_off, total_rows):
    tm = math.gcd(seq_len, TM_DFT)
    tk = math.gcd(seq_len, TK_DFT)
    kh = seq_len // tk
    scale = 1.0 / math.sqrt(seq_len * (GROUP_W // FN_GROUPS))
    in_specs = [
        pl.BlockSpec((tm, tk), lambda b, i, k: (i, k)),
        pl.BlockSpec((tk, GROUP_W), lambda b, i, k: (row_off // tk + b * kh + k % kh, k // kh)),
    ]
    args = [fmat, z]
    aliases = {}
    if prev_out is not None:
        in_specs.append(pl.BlockSpec(memory_space=pl.ANY))
        args.append(prev_out)
        aliases = {2: 0}
    return pl.pallas_call(
        functools.partial(_seqdft_kernel, scale=scale),
        out_shape=jax.ShapeDtypeStruct((total_rows, GROUP_W), F32),
        grid=(n_seq, seq_len // tm, 2 * kh),
        in_specs=in_specs,
        out_specs=pl.BlockSpec((tm, GROUP_W), lambda b, i, k: (row_off // tm + b * (seq_len // tm) + i, 0)),
        scratch_shapes=[pltpu.VMEM((tm, GROUP_W), F32)],
        input_output_aliases=aliases,
        compiler_params=_cp(("arbitrary", "arbitrary", "arbitrary")),
        name="seqdft",
    )(*args)


def _ssd_kernel(*refs, direction, final):
    if final:
        (xbc_ref, dt_ref, a_ref, bias_ref, e_ref, tri_ref, yf_ref, z_ref, dsk_ref, ng_ref,
         o_ref, st_scr) = refs
    else:
        (xbc_ref, dt_ref, a_ref, bias_ref, e_ref, tri_ref, o_ref, st_scr) = refs
    q = SSD_CHUNK
    gw = SSD_HEADS // SSD_GROUPS * SSD_HEAD_DIM
    hpg = SSD_HEADS // SSD_GROUPS
    n = SSD_STATE

    @pl.when(pl.program_id(1) == 0)
    def _():
        st_scr[...] = jnp.zeros_like(st_scr)

    e01 = e_ref[...]
    dtp = _softplus(dt_ref[...] + bias_ref[...])
    a = dtp * a_ref[...]
    acs = _dot_01x(tri_ref[...], a)
    acs_t = acs.T
    acs_e = _dot_x01(acs, e01)
    dt_e = _dot_x01(dtp, e01)
    edge_row = q - 1 if direction == 0 else 0
    edge = acs_e[edge_row:edge_row + 1, :]
    x = xbc_ref[:, :SSD_HEADS * SSD_HEAD_DIM]
    xdt = x * dt_e
    xdt_b = xdt.astype(BF16)
    w_b = (xdt * jnp.exp(edge - acs_e)).astype(BF16)
    e_acs = jnp.exp(acs_e)
    e_edge = jnp.exp(edge)

    li = lax.broadcasted_iota(I32, (q, q), 0)
    si = lax.broadcasted_iota(I32, (q, q), 1)
    causal = (li >= si) if direction == 0 else (li <= si)
    lane = lax.broadcasted_iota(I32, (q, 2 * SSD_HEAD_DIM), 1)
    first_half = lane < SSD_HEAD_DIM

    x_w = SSD_HEADS * SSD_HEAD_DIM
    ys = []
    for g in range(SSD_GROUPS):
        bg = xbc_ref[:, x_w + g * n:x_w + (g + 1) * n].astype(BF16)
        cg = xbc_ref[:, x_w + SSD_GROUPS * n + g * n:x_w + SSD_GROUPS * n + (g + 1) * n].astype(BF16)
        gmat = _dot_nt(cg, bg)
        s_in = st_scr[g]
        y_off = _dot(cg, s_in.astype(BF16)) * e_acs[:, g * gw:(g + 1) * gw]
        y_diag = []
        for pair in range(hpg // 2):
            ms = []
            for r in (2 * pair, 2 * pair + 1):
                ln = direction * SSD_HEADS + g * hpg + r
                seg = acs[:, ln:ln + 1] - acs_t[ln:ln + 1, :]
                ms.append((gmat * jnp.where(causal, jnp.exp(seg), 0.0)).astype(BF16))
            c0 = g * gw + pair * 2 * SSD_HEAD_DIM
            xp = xdt_b[:, c0:c0 + 2 * SSD_HEAD_DIM]
            zero = jnp.zeros_like(xp)
            rhs = jnp.concatenate([jnp.where(first_half, xp, zero), jnp.where(first_half, zero, xp)], axis=0)
            y_diag.append(_dot(jnp.concatenate(ms, axis=1), rhs))
        ys.append(jnp.concatenate(y_diag, axis=1) + y_off)
        st_new = _dot_tn(bg, w_b[:, g * gw:(g + 1) * gw])
        st_scr[g] = e_edge[:, g * gw:(g + 1) * gw] * s_in + st_new
    y = jnp.concatenate(ys, axis=1)
    if final:
        tot = x * dsk_ref[...] + yf_ref[...] + y
        o_ref[...] = _rms(tot * _silu(z_ref[...]), ng_ref[...])
    else:
        o_ref[...] = y


def _ssd_pass(xbc, dt, u_all, z_col_block, a_row, bias_row, e01, tri, yf, dsk, norm_g, dims, direction):
    t = xbc.shape[0]
    b, seq, ctx, nl = dims["B"], dims["L"], dims["C"], dims["NL"]
    q = SSD_CHUNK
    n_c, n_l = ctx // q, seq // q
    final = yf is not None
    width = SSD_HEADS * SSD_HEAD_DIM

    def blk(bi, s):
        c_idx = s if direction == 0 else n_c - 1 - s
        l_idx = (s - n_c) if direction == 0 else n_l - 1 - (s - n_c)
        return jnp.where(s < n_c, nl // q + bi * n_c + c_idx, bi * n_l + l_idx)

    def const(shape):
        return pl.BlockSpec(shape, lambda bi, s: (0,) * len(shape))

    out_spec = pl.BlockSpec((q, width), lambda bi, s: (blk(bi, s), 0))
    in_specs = [
        pl.BlockSpec((q, xbc.shape[1]), lambda bi, s: (blk(bi, s), 0)),
        pl.BlockSpec((q, LANES), lambda bi, s: (blk(bi, s), 0)),
        const((1, LANES)), const((1, LANES)), const(e01.shape), const(tri.shape),
    ]
    args = [xbc, dt, a_row, bias_row, e01, tri]
    if final:
        in_specs += [out_spec, pl.BlockSpec((q, width), lambda bi, s: (blk(bi, s), z_col_block)),
                     const((1, width)), const((1, width))]
        args += [yf, u_all, dsk, norm_g.reshape(1, width)]
    kernel = functools.partial(_ssd_kernel, direction=direction, final=final)
    return pl.pallas_call(
        kernel,
        out_shape=jax.ShapeDtypeStruct((t, width), F32),
        grid=(b, n_c + n_l),
        in_specs=in_specs,
        out_specs=out_spec,
        scratch_shapes=[pltpu.VMEM((SSD_GROUPS, SSD_STATE, width // SSD_GROUPS), F32)],
        compiler_params=_cp(("arbitrary", "arbitrary")),
        name="ssd_dir%d" % direction,
    )(*args)


def _outproj_kernel(att_ref, lru_ref, four_ref, ssd_ref, wo_ref, x_ref, mod_ref, n2_ref, rw_ref, rb_ref,
                    x1_ref, tok_ref, mi_ref, mg_ref, cnt_ref, carry):
    @pl.when(pl.program_id(0) == 0)
    def _():
        carry[...] = jnp.zeros_like(carry)

    acc = _dot(att_ref[...].astype(BF16), wo_ref[0])
    acc += _dot(lru_ref[...].astype(BF16), wo_ref[1])
    acc += _dot(four_ref[...].astype(BF16), wo_ref[2])
    acc += _dot(ssd_ref[...].astype(BF16), wo_ref[3])
    x1 = x_ref[...] + mod_ref[2:3, :] * acc
    x1_ref[...] = x1
    tok = _rms(x1, n2_ref[...]) * (1.0 + mod_ref[4:5, :]) + mod_ref[3:4, :]
    tok_ref[...] = tok

    logits = _dot_hi(tok, rw_ref[...]) + rb_ref[...]
    tm = logits.shape[0]
    lane = lax.broadcasted_iota(I32, logits.shape, 1).astype(F32)
    work = logits
    vals, idxs = [], []
    for _ in range(TOP_K):
        m = jnp.max(work, axis=-1, keepdims=True)
        idx = jnp.min(jnp.where(work == m, lane, float(LANES)), axis=-1, keepdims=True)
        vals.append(m)
        idxs.append(idx)
        work = jnp.where(lane == idx, -jnp.inf, work)
    es = [jnp.exp(v - vals[0]) for v in vals]
    denom = es[0] + es[1] + es[2] + es[3]
    sel = jnp.zeros(logits.shape, F32)
    for idx in idxs:
        sel = sel + (lane == idx).astype(F32)
    ri = lax.broadcasted_iota(I32, (tm, tm), 0)
    ci = lax.broadcasted_iota(I32, (tm, tm), 1)
    strict_lower = (ci < ri).astype(BF16)
    cum = _dot(strict_lower, sel.astype(BF16)) + carry[...]
    carry[...] = carry[...] + jnp.sum(sel, axis=0, keepdims=True)
    cnt_ref[...] = jnp.broadcast_to(carry[...], cnt_ref.shape)
    mi = jnp.zeros(logits.shape, F32)
    mg = jnp.zeros(logits.shape, F32)
    for k in range(TOP_K):
        rank = jnp.sum(jnp.where(lane == idxs[k], cum, 0.0), axis=-1, keepdims=True)
        mi = jnp.where(lane == float(k), idxs[k], mi)
        mi = jnp.where(lane == float(TOP_K + k), rank, mi)
        mg = jnp.where(lane == float(k), es[k] / denom, mg)
    mi_ref[...] = mi.astype(I32)
    mg_ref[...] = mg


def _outproj(att, lru, four, ssd, wo4, x, mod6, norm2, rw, rb, dims):
    t, d = x.shape
    tm = math.gcd(math.gcd(dims["L"], dims["TC"]), TM_TOKEN)
    mod_row = dims["mod_row"]
    feat = pl.BlockSpec((tm, GROUP_W), lambda i: (i, 0))
    row_d = pl.BlockSpec((tm, d), lambda i: (i, 0))
    row_l = pl.BlockSpec((tm, LANES), lambda i: (i, 0))
    return pl.pallas_call(
        _outproj_kernel,
        out_shape=(jax.ShapeDtypeStruct((t, d), F32), jax.ShapeDtypeStruct((t, d), F32),
                   jax.ShapeDtypeStruct((t, LANES), I32), jax.ShapeDtypeStruct((t, LANES), F32),
                   jax.ShapeDtypeStruct((SUBLANES, LANES), F32)),
        grid=(t // tm,),
        in_specs=[feat, feat, feat, feat,
                  pl.BlockSpec(wo4.shape, lambda i: (0, 0, 0)),
                  row_d,
                  pl.BlockSpec((None, 6, d), lambda i: (mod_row(i * tm), 0, 0)),
                  pl.BlockSpec((1, d), lambda i: (0, 0)),
                  pl.BlockSpec((d, LANES), lambda i: (0, 0)),
                  pl.BlockSpec((1, LANES), lambda i: (0, 0))],
        out_specs=(row_d, row_d, row_l, row_l, pl.BlockSpec((SUBLANES, LANES), lambda i: (0, 0))),
        scratch_shapes=[pltpu.VMEM((1, LANES), F32)],
        compiler_params=_cp(("arbitrary",)),
        name="outproj_router",
    )(att, lru, four, ssd, wo4, x, mod6, norm2.reshape(1, d), rw, rb)


def _gu_split_kernel(w_ref, p_ref, g_ref, u_ref):
    for c in range(g_ref.shape[1] // LANES):
        wb = w_ref[:, c * 2 * LANES:(c + 1) * 2 * LANES].astype(BF16)
        y = _dot(wb, p_ref[...])
        g_ref[:, c * LANES:(c + 1) * LANES] = y[:, :LANES].astype(BF16)
        u_ref[:, c * LANES:(c + 1) * LANES] = y[:, LANES:].astype(BF16)


def _gu_split(w_gu):
    g, d, ff2 = w_gu.shape
    ff = ff2 // 2
    tn = math.gcd(ff, 2 * LANES)
    perm = np.zeros((2 * LANES, 2 * LANES), np.float32)
    for j in range(LANES):
        perm[2 * j, j] = 1.0
        perm[2 * j + 1, LANES + j] = 1.0
    out = jax.ShapeDtypeStruct((g, d, ff), BF16)
    return pl.pallas_call(
        _gu_split_kernel,
        out_shape=(out, out),
        grid=(g, ff // tn),
        in_specs=[pl.BlockSpec((None, d, 2 * tn), lambda e, j: (e, 0, j)),
                  pl.BlockSpec((2 * LANES, 2 * LANES), lambda e, j: (0, 0))],
        out_specs=(pl.BlockSpec((None, d, tn), lambda e, j: (e, 0, j)),
                   pl.BlockSpec((None, d, tn), lambda e, j: (e, 0, j))),
        compiler_params=_cp(("arbitrary", "arbitrary")),
        name="gu_split",
    )(w_gu, jnp.asarray(perm, dtype=BF16))


def _moe_kernel(be_ref, nu_ref, src0_ref, src_ref, dst_ref, tok_ref, wg_ref, wu_ref, bg_ref, bu_ref, wd_ref,
                bd_ref, ys_ref, xbuf0, xbuf1, ybuf0, ybuf1, gsem, ssem):
    del be_ref
    i = pl.program_id(0)
    n_used = nu_ref[0]
    xbuf = (xbuf0, xbuf1)
    ybuf = (ybuf0, ybuf1)
    bm = xbuf0.shape[0]

    def gather(idx_ref, j, sl):
        return pltpu.make_async_copy(tok_ref.at[pl.ds(idx_ref[j], 1)], xbuf[sl].at[pl.ds(j, 1)], gsem.at[sl])

    def scatter(j, sl):
        return pltpu.make_async_copy(ybuf[sl].at[pl.ds(j, 1)], ys_ref.at[pl.ds(dst_ref[j], 1)], ssem.at[sl])

    @pl.when(i == 0)
    def _():
        ybuf1[...] = jnp.zeros_like(ybuf1)

        def first(j, c):
            gather(src0_ref, j, 0).start()
            return c

        lax.fori_loop(0, bm, first, 0)

    for slot in (0, 1):
        other = 1 - slot
        mine = i % 2 == slot

        @pl.when(mine & (i <= n_used + 1))
        def _():
            pltpu.make_async_copy(tok_ref.at[pl.ds(0, bm)], xbuf[slot], gsem.at[slot]).wait()

        @pl.when(mine & (i >= 1) & (i <= n_used + 1))
        def _():
            pltpu.make_async_copy(ybuf[slot], ys_ref.at[pl.ds(0, bm)], ssem.at[slot]).wait()

        @pl.when(mine & (i <= n_used))
        def _():
            for j in range(bm):
                gather(src_ref, j, other).start()
                scatter(j, other).start()
            x = xbuf[slot][...].astype(BF16)
            g = _dot(x, wg_ref[...]) + bg_ref[...]
            u = _dot(x, wu_ref[...]) + bu_ref[...]
            gate = jnp.minimum(g, SWIGLU_LIMIT)
            up = jnp.clip(u, -SWIGLU_LIMIT, SWIGLU_LIMIT)
            act = gate * _sigmoid(SWIGLU_ALPHA * gate) * (up + 1.0)
            ybuf[slot][...] = _dot(act.astype(BF16), wd_ref[...]) + bd_ref[...]


def _moe_experts(tok, src_rows, dst_rows, blk_e, n_used, wg, wu, bg, bu, wd, bd, n_blk):
    t, d = tok.shape
    bm = BM_MOE
    ff = wg.shape[2]

    def wsel(i, be, nu):
        return (be[i], 0, 0)

    smem = functools.partial(pl.BlockSpec, memory_space=pltpu.SMEM)
    grid_spec = pltpu.PrefetchScalarGridSpec(
        num_scalar_prefetch=2,
        grid=(n_blk + 2,),
        in_specs=[
            smem((bm,), lambda i, be, nu: (0,)),
            smem((bm,), lambda i, be, nu: (jnp.minimum(i + 1, n_blk + 1),)),
            smem((bm,), lambda i, be, nu: (i,)),
            pl.BlockSpec(memory_space=pl.ANY),
            pl.BlockSpec((None, d, ff), wsel),
            pl.BlockSpec((None, d, ff), wsel),
            pl.BlockSpec((None, 1, ff), wsel),
            pl.BlockSpec((None, 1, ff), wsel),
            pl.BlockSpec((None, ff, d), wsel),
            pl.BlockSpec((None, 1, d), wsel),
        ],
        out_specs=pl.BlockSpec(memory_space=pl.ANY),
        scratch_shapes=[pltpu.VMEM((bm, d), F32), pltpu.VMEM((bm, d), F32),
                        pltpu.VMEM((bm, d), F32), pltpu.VMEM((bm, d), F32),
                        pltpu.SemaphoreType.DMA((2,)), pltpu.SemaphoreType.DMA((2,))],
    )
    return pl.pallas_call(
        _moe_kernel,
        out_shape=jax.ShapeDtypeStruct((TOP_K * t + bm, d), F32),
        grid_spec=grid_spec,
        compiler_params=_cp(("arbitrary",)),
        name="moe_experts",
    )(blk_e, n_used, src_rows, src_rows, dst_rows, tok, wg, wu, bg, bu, wd, bd)


def _combine_kernel(y0_ref, y1_ref, y2_ref, y3_ref, x1_ref, mg_ref, mod_ref, o_ref):
    mg = mg_ref[...]
    ffn = mg[:, 0:1] * y0_ref[...] + mg[:, 1:2] * y1_ref[...]
    ffn = ffn + mg[:, 2:3] * y2_ref[...] + mg[:, 3:4] * y3_ref[...]
    o_ref[...] = x1_ref[...] + mod_ref[5:6, :] * ffn


def _combine(ys, x1, mg, mod6, dims):
    t, d = x1.shape
    tm = math.gcd(math.gcd(dims["L"], dims["TC"]), TM_TOKEN)
    mod_row = dims["mod_row"]
    nt = t // tm

    def plane(k):
        return pl.BlockSpec((tm, d), lambda i: (k * nt + i, 0))

    return pl.pallas_call(
        _combine_kernel,
        out_shape=jax.ShapeDtypeStruct((t, d), F32),
        grid=(nt,),
        in_specs=[plane(0), plane(1), plane(2), plane(3),
                  pl.BlockSpec((tm, d), lambda i: (i, 0)),
                  pl.BlockSpec((tm, LANES), lambda i: (i, 0)),
                  pl.BlockSpec((None, 6, d), lambda i: (mod_row(i * tm), 0, 0))],
        out_specs=pl.BlockSpec((tm, d), lambda i: (i, 0)),
        compiler_params=_cp(("arbitrary",)),
        name="moe_combine",
    )(ys, ys, ys, ys, x1, mg, mod6)


def _rope_tables(seq, ctx):
    rows = seq // GRID_W
    row = np.repeat(np.arange(rows), GRID_W).astype(np.float32)
    col = np.tile(np.arange(GRID_W), rows).astype(np.float32)
    pairs = HEAD_DIM // 4
    inv = (ROPE_THETA ** (-np.arange(pairs, dtype=np.float32) / pairs)).astype(np.float32)
    ang = np.concatenate([row[:, None] * inv, col[:, None] * inv], axis=-1)
    ang = np.concatenate([ang, ang], axis=-1).astype(np.float32)
    sign = np.where(np.arange(HEAD_DIM) < HEAD_DIM // 2, -1.0, 1.0).astype(np.float32)
    cos = np.concatenate([np.ones((ctx, HEAD_DIM), np.float32), np.cos(ang)], axis=0)
    sin = np.concatenate([np.zeros((ctx, HEAD_DIM), np.float32), np.sin(ang) * sign], axis=0)
    return jnp.asarray(cos), jnp.asarray(sin)


def _dft_cos_sin(n):
    k = np.arange(n, dtype=np.int64)
    ang = 2.0 * np.pi * ((k[:, None] * k[None, :]) % n).astype(np.float64) / n
    return np.cos(ang), np.sin(ang)


def _seq_dft_matrix(n):
    c, s = _dft_cos_sin(n)
    return jnp.asarray(np.concatenate([c, -s], axis=1), dtype=BF16)


def _ssd_constants(direction):
    e = np.zeros((LANES, SSD_HEADS * SSD_HEAD_DIM), np.float32)
    for h in range(SSD_HEADS):
        e[direction * SSD_HEADS + h, h * SSD_HEAD_DIM:(h + 1) * SSD_HEAD_DIM] = 1.0
    li = np.arange(SSD_CHUNK)
    tri = (li[:, None] >= li[None, :]) if direction == 0 else (li[:, None] <= li[None, :])
    return jnp.asarray(e, dtype=BF16), jnp.asarray(tri.astype(np.float32), dtype=BF16)


def _lane_row(v):
    flat = v.reshape(1, -1).astype(F32)
    return jnp.pad(flat, ((0, 0), (0, LANES - flat.shape[1])))


def kernel(x, c, ctx, c_ctx, w_mod, b_mod, norm1, norm2, w_in, w_out, q_norm, k_norm, lru_conv_w, lru_conv_b,
           lru_wa, lru_ba, lru_wx, lru_bx, lru_lam, ssd_conv_w, ssd_conv_b, ssd_a_log, ssd_dt_bias, ssd_d,
           ssd_norm, router_w, router_b, exp_w_gu, exp_b_gu, exp_w_dn, exp_b_dn):
    b, seq, d = x.shape
    n_ctx = ctx.shape[1]
    depth = w_in.shape[0]
    n_exp = router_w.shape[2]
    ff = exp_w_dn.shape[2]
    nl, tc_rows = b * seq, b * n_ctx
    t = nl + tc_rows
    assert seq % SSD_CHUNK == 0 and n_ctx % SSD_CHUNK == 0 and seq % n_ctx == 0 and nl % n_ctx == 0
    assert seq % GRID_W == 0 and n_exp <= LANES and b + 1 <= 2 * SUBLANES

    def mod_row(r):
        return jnp.where(r < nl, r // seq, b)

    dims = dict(B=b, L=seq, C=n_ctx, NL=nl, TC=tc_rows, mod_row=mod_row)

    xa = jnp.concatenate([x.reshape(nl, d), ctx.reshape(tc_rows, d)], axis=0)
    cc = jnp.concatenate([c, c_ctx[None, :], jnp.zeros((2 * SUBLANES - b - 1, d), F32)], axis=0)
    mods = _mod_all(cc, w_mod, b_mod).reshape(depth, 2 * SUBLANES, 6, d)

    cos_t, sin_t = _rope_tables(seq, n_ctx)
    c128, s128 = _dft_cos_sin(GROUP_W // FN_GROUPS)
    c128, s128 = jnp.asarray(c128, dtype=BF16), jnp.asarray(s128, dtype=BF16)
    f_lat, f_ctx = _seq_dft_matrix(seq), _seq_dft_matrix(n_ctx)
    ssd_const = [_ssd_constants(0), _ssd_constants(1)]

    n_main = w_in.shape[2] - 2 * SSD_HEADS
    col_lx, col_lg, col_f, col_z, col_xbc = 1024, 1536, 2048, 2560, 3072
    xbc_w = n_main - col_xbc

    assert (t * TOP_K) % BM_MOE == 0
    n_blk = t * TOP_K // BM_MOE + n_exp
    wg_all, wu_all = _gu_split(exp_w_gu.reshape(depth * n_exp, d, 2 * ff))
    wg_all, wu_all = wg_all.reshape(depth, n_exp, d, ff), wu_all.reshape(depth, n_exp, d, ff)

    for l in range(depth):
        w_main = w_in[l, :, :n_main].astype(BF16)
        w_dt = jnp.pad(w_in[l, :, n_main:], ((0, 0), (0, LANES - 2 * SSD_HEADS))).astype(BF16)
        mod6 = mods[l]
        u, dt = _inproj(xa, mod6, norm1[l], w_main, w_dt, dims)

        qn, kn, vb = _qkprep(u, cos_t, sin_t, q_norm[l], k_norm[l], dims)
        att = _attention(qn, kn, vb, dims)

        lxc = _dwconv(u, col_lx, GROUP_W, lru_conv_w[l], lru_conv_b[l], dims, act=False)
        wa, wx = lru_wa[l].astype(BF16), lru_wx[l].astype(BF16)
        hb = _lru_pass(lxc, u, None, wa[1], lru_ba[l, 1], wx[1], lru_bx[l, 1], lru_lam[l, 1], None, dims, 1)
        lru = _lru_pass(lxc, u, col_lg // GROUP_W, wa[0], lru_ba[l, 0], wx[0], lru_bx[l, 0], lru_lam[l, 0],
                        hb, dims, 0)

        z = _chdft(u, col_f // GROUP_W, c128, s128, dims)
        four = _seqdft(f_lat, z, jnp.zeros((t, GROUP_W), F32), b, seq, 0, t)
        four = _seqdft(f_ctx, z, four, b, n_ctx, nl, t)

        xbc = _dwconv(u, col_xbc, xbc_w, ssd_conv_w[l], ssd_conv_b[l], dims, act=True)
        a_row = _lane_row(-jnp.exp(ssd_a_log[l]))
        bias_row = _lane_row(ssd_dt_bias[l])
        dsk = jnp.repeat(ssd_d[l], SSD_HEAD_DIM).reshape(1, -1)
        yf = _ssd_pass(xbc, dt, u, None, a_row, bias_row, *ssd_const[0], None, None, None, dims, 0)
        ssd = _ssd_pass(xbc, dt, u, col_z // GROUP_W, a_row, bias_row, *ssd_const[1], yf, dsk, ssd_norm[l],
                        dims, 1)

        wo4 = w_out[l].astype(BF16).reshape(4, GROUP_W, d)
        rw = jnp.pad(router_w[l], ((0, 0), (0, LANES - n_exp)))
        rb = jnp.pad(router_b[l], (0, LANES - n_exp), constant_values=NEG_BIG).reshape(1, LANES)
        x1, tok, mi, mg, cnt = _outproj(att, lru, four, ssd, wo4, xa, mod6, norm2[l], rw, rb, dims)

        counts = cnt[0, :n_exp].astype(I32)
        pcounts = (counts + BM_MOE - 1) // BM_MOE * BM_MOE
        pends = jnp.cumsum(pcounts)
        pstarts = pends - pcounts
        dest = (pstarts[mi[:, :TOP_K]] + mi[:, TOP_K:2 * TOP_K]).reshape(-1)
        blk_start = jnp.arange(n_blk + 2, dtype=I32) * BM_MOE
        blk_e = jnp.minimum(jnp.sum((pends[None, :] <= blk_start[:, None]).astype(I32), axis=1), n_exp - 1)
        n_used = (pends[-1:] // BM_MOE).astype(I32)
        n_slot = (n_blk + 2) * BM_MOE
        inv = jnp.full((n_slot,), -1, I32).at[dest].set(jnp.arange(t * TOP_K, dtype=I32), unique_indices=True)
        spare = TOP_K * t + jnp.arange(n_slot, dtype=I32) % BM_MOE
        src_rows = jnp.where(inv >= 0, inv // TOP_K, 0)
        dst_rows = jnp.where(inv >= 0, (inv % TOP_K) * t + inv // TOP_K, spare)
        dst_rows = jnp.concatenate([spare[:BM_MOE], dst_rows[:n_slot - BM_MOE]])

        bgu = exp_b_gu[l].reshape(n_exp, 1, ff, 2)
        ys = _moe_experts(tok, src_rows, dst_rows, blk_e, n_used, wg_all[l], wu_all[l], bgu[..., 0], bgu[..., 1],
                          exp_w_dn[l].astype(BF16), exp_b_dn[l].reshape(n_exp, 1, d), n_blk)
        xa = _combine(ys, x1, mg, mod6, dims)

    return xa[:nl].reshape(b, seq, d)
```

```python
import functools
import math

import jax
import jax.numpy as jnp
import numpy as np
from jax import lax
from jax.experimental import pallas as pl
from jax.experimental.pallas import tpu as pltpu

F32 = jnp.float32
BF16 = jnp.bfloat16
I32 = jnp.int32

HEAD_DIM = 128
ATT_Q_HEADS = 4
ATT_KV_HEADS = 2
GRID_W = 64
ROPE_THETA = 10000.0
CONV_W = 4
LRU_BLOCKS = 4
LRU_C = 8.0
FN_GROUPS = 4
SSD_HEAD_DIM = 64
SSD_HEADS = 8
SSD_GROUPS = 2
SSD_STATE = 128
SSD_CHUNK = 128
TOP_K = 4
SWIGLU_LIMIT = 7.0
SWIGLU_ALPHA = 1.702
EPS = 1e-6
GROUP_W = 512

LANES = 128
SUBLANES = 8
VMEM_LIMIT = 56 * 1024 * 1024

NEG_BIG = -1e30

TM_INPROJ = 1024
TN_INPROJ = 512
TM_TOKEN = 256
TQ_ATT = 256
TK_ATT = 512
TC_SCAN = 256
TM_DFT = 512
BM_MOE = 512


def _cp(sem, vmem=VMEM_LIMIT):
    return pltpu.CompilerParams(dimension_semantics=sem, vmem_limit_bytes=vmem)


def _dot(a, b):
    return jnp.dot(a, b, preferred_element_type=F32)


def _dot_nt(a, b):
    return lax.dot_general(a, b, (((1,), (1,)), ((), ())), preferred_element_type=F32)


def _dot_tn(a, b):
    return lax.dot_general(a, b, (((0,), (0,)), ((), ())), preferred_element_type=F32)


def _split3(x):
    hi = x.astype(BF16)
    r = x - hi.astype(F32)
    mid = r.astype(BF16)
    lo = (r - mid.astype(F32)).astype(BF16)
    return hi, mid, lo


def _dot_x01(x, m01):
    hi, mid, lo = _split3(x)
    return _dot(hi, m01) + _dot(mid, m01) + _dot(lo, m01)


def _dot_01x(m01, x):
    hi, mid, lo = _split3(x)
    return _dot(m01, hi) + _dot(m01, mid) + _dot(m01, lo)


def _dot_hi(a, b):
    ah = a.astype(BF16)
    al = (a - ah.astype(F32)).astype(BF16)
    bh = b.astype(BF16)
    bl = (b - bh.astype(F32)).astype(BF16)
    return _dot(ah, bh) + _dot(ah, bl) + _dot(al, bh)


def _sigmoid(x):
    return 1.0 / (1.0 + jnp.exp(-x))


def _silu(x):
    return x * _sigmoid(x)


def _softplus(x):
    return jnp.maximum(x, 0.0) + jnp.log(1.0 + jnp.exp(-jnp.abs(x)))


def _gelu_tanh(x):
    c = math.sqrt(2.0 / math.pi)
    return 0.5 * x * (1.0 + jnp.tanh(c * (x + 0.044715 * (x * x * x))))


def _rms(x, g):
    return x * lax.rsqrt(jnp.mean(x * x, axis=-1, keepdims=True) + EPS) * g


def _mod_kernel(cc_ref, w_ref, b_ref, o_ref):
    s = _silu(cc_ref[...])
    o_ref[...] = _dot_hi(s, w_ref[...]) + b_ref[...]


def _mod_all(cc, w_mod, b_mod):
    n_layers, d, d6 = w_mod.shape
    tn = 1024
    rows = cc.shape[0]
    return pl.pallas_call(
        _mod_kernel,
        out_shape=jax.ShapeDtypeStruct((n_layers, rows, d6), F32),
        grid=(n_layers, d6 // tn),
        in_specs=[
            pl.BlockSpec((rows, d), lambda l, j: (0, 0)),
            pl.BlockSpec((None, d, tn), lambda l, j: (l, 0, j)),
            pl.BlockSpec((None, 1, tn), lambda l, j: (l, 0, j)),
        ],
        out_specs=pl.BlockSpec((None, rows, tn), lambda l, j: (l, 0, j)),
        compiler_params=_cp(("arbitrary", "arbitrary")),
        name="mod_all",
    )(cc, w_mod, b_mod.reshape(n_layers, 1, d6))


def _inproj_kernel(x_ref, mod_ref, n1_ref, w_ref, wdt_ref, u_ref, dt_ref, h_scr):
    @pl.when(pl.program_id(1) == 0)
    def _():
        x = x_ref[...]
        h = _rms(x, n1_ref[...]) * (1.0 + mod_ref[1:2, :]) + mod_ref[0:1, :]
        hb = h.astype(BF16)
        h_scr[...] = hb
        dt_ref[...] = _dot(hb, wdt_ref[...])

    u_ref[...] = _dot(h_scr[...], w_ref[...])


def _inproj(x, mod6, norm1, w_main, w_dt, dims):
    t, d = x.shape
    n_main = w_main.shape[1]
    tm = math.gcd(math.gcd(dims["L"], dims["TC"]), TM_INPROJ)
    tn = TN_INPROJ
    mod_row = dims["mod_row"]
    return pl.pallas_call(
        _inproj_kernel,
        out_shape=(jax.ShapeDtypeStruct((t, n_main), F32), jax.ShapeDtypeStruct((t, LANES), F32)),
        grid=(t // tm, n_main // tn),
        in_specs=[
            pl.BlockSpec((tm, d), lambda i, j: (i, 0)),
            pl.BlockSpec((None, 6, d), lambda i, j: (mod_row(i * tm), 0, 0)),
            pl.BlockSpec((1, d), lambda i, j: (0, 0)),
            pl.BlockSpec((d, tn), lambda i, j: (0, j)),
            pl.BlockSpec((d, LANES), lambda i, j: (0, 0)),
        ],
        out_specs=(pl.BlockSpec((tm, tn), lambda i, j: (i, j)),
                   pl.BlockSpec((tm, LANES), lambda i, j: (i, 0))),
        scratch_shapes=[pltpu.VMEM((tm, d), BF16)],
        compiler_params=_cp(("arbitrary", "arbitrary")),
        name="inproj",
    )(x, mod6, norm1.reshape(1, d), w_main, w_dt)


def _qkprep_kernel(q_ref, k_ref, v_ref, cos_ref, sin_ref, qn_ref, kn_ref, qo_ref, ko_ref, vo_ref):
    cos = cos_ref[...]
    sin = sin_ref[...]

    def prep(xh, g):
        y = _rms(xh, g)
        return y * cos + pltpu.roll(y, HEAD_DIM // 2, 1) * sin

    for h in range(ATT_Q_HEADS):
        sl = slice(h * HEAD_DIM, (h + 1) * HEAD_DIM)
        qo_ref[:, sl] = prep(q_ref[:, sl], qn_ref[...]).astype(BF16)
    for h in range(ATT_KV_HEADS):
        sl = slice(h * HEAD_DIM, (h + 1) * HEAD_DIM)
        ko_ref[:, sl] = prep(k_ref[:, sl], kn_ref[...]).astype(BF16)
    vo_ref[...] = v_ref[...].astype(BF16)


def _qkprep(u, cos_t, sin_t, q_norm, k_norm, dims):
    t = u.shape[0]
    tp = math.gcd(math.gcd(dims["L"], dims["C"]), TM_TOKEN)
    nl, seq, ctx = dims["NL"], dims["L"], dims["C"]

    def tab(i):
        r = i * tp
        return jnp.where(r < nl, ctx // tp + (r % seq) // tp, 0)

    qw, kw = ATT_Q_HEADS * HEAD_DIM, ATT_KV_HEADS * HEAD_DIM
    return pl.pallas_call(
        _qkprep_kernel,
        out_shape=(jax.ShapeDtypeStruct((t, qw), BF16), jax.ShapeDtypeStruct((t, kw), BF16),
                   jax.ShapeDtypeStruct((t, kw), BF16)),
        grid=(t // tp,),
        in_specs=[
            pl.BlockSpec((tp, qw), lambda i: (i, 0)),
            pl.BlockSpec((tp, kw), lambda i: (i, qw // kw)),
            pl.BlockSpec((tp, kw), lambda i: (i, qw // kw + 1)),
            pl.BlockSpec((tp, HEAD_DIM), lambda i: (tab(i), 0)),
            pl.BlockSpec((tp, HEAD_DIM), lambda i: (tab(i), 0)),
            pl.BlockSpec((1, HEAD_DIM), lambda i: (0, 0)),
            pl.BlockSpec((1, HEAD_DIM), lambda i: (0, 0)),
        ],
        out_specs=(pl.BlockSpec((tp, qw), lambda i: (i, 0)),
                   pl.BlockSpec((tp, kw), lambda i: (i, 0)),
                   pl.BlockSpec((tp, kw), lambda i: (i, 0))),
        compiler_params=_cp(("arbitrary",)),
        name="qkprep",
    )(u, u, u, cos_t, sin_t, q_norm.reshape(1, HEAD_DIM), k_norm.reshape(1, HEAD_DIM))


def _attn_kernel(q_ref, kc_ref, vc_ref, kl_ref, vl_ref, o_ref, *, n_lat_q, n_lat_chunks, tk):
    tq = q_ref.shape[0]
    scale = HEAD_DIM ** -0.5
    q2 = jnp.concatenate([q_ref[:, :HEAD_DIM], q_ref[:, HEAD_DIM:]], axis=0)

    def step(k, v, carry):
        m, l, acc = carry
        s = _dot_nt(q2, k) * scale
        m_new = jnp.maximum(m, jnp.max(s, axis=-1, keepdims=True))
        alpha = jnp.exp(m - m_new)
        p = jnp.exp(s - m_new)
        l = alpha * l + jnp.sum(p, axis=-1, keepdims=True)
        acc = alpha * acc + _dot(p.astype(BF16), v)
        return m_new, l, acc

    init = (jnp.full((2 * tq, 1), NEG_BIG, F32), jnp.zeros((2 * tq, 1), F32),
            jnp.zeros((2 * tq, HEAD_DIM), F32))
    carry = step(kc_ref[...], vc_ref[...], init)

    def body(j, c):
        off = pl.multiple_of(j * tk, tk)
        return step(kl_ref[pl.ds(off, tk), :], vl_ref[pl.ds(off, tk), :], c)

    n = jnp.where(pl.program_id(2) < n_lat_q, n_lat_chunks, 0)
    m, l, acc = lax.fori_loop(0, n, body, carry)
    o = acc / l
    o_ref[:, :HEAD_DIM] = o[:tq]
    o_ref[:, HEAD_DIM:] = o[tq:]


def _attention(qn, kn, vb, dims):
    t = qn.shape[0]
    b, seq, ctx, nl = dims["B"], dims["L"], dims["C"], dims["NL"]
    tq = math.gcd(math.gcd(seq, ctx), TQ_ATT)
    tk = math.gcd(seq, TK_ATT)
    n_lat_q, n_ctx_q = seq // tq, ctx // tq
    rep = ATT_Q_HEADS // ATT_KV_HEADS

    def qrow(bi, i):
        return jnp.where(i < n_lat_q, bi * n_lat_q + i, nl // tq + bi * n_ctx_q + (i - n_lat_q))

    kernel = functools.partial(_attn_kernel, n_lat_q=n_lat_q, n_lat_chunks=seq // tk, tk=tk)
    return pl.pallas_call(
        kernel,
        out_shape=jax.ShapeDtypeStruct((t, ATT_Q_HEADS * HEAD_DIM), F32),
        grid=(b, ATT_KV_HEADS, n_lat_q + n_ctx_q),
        in_specs=[
            pl.BlockSpec((tq, rep * HEAD_DIM), lambda bi, g, i: (qrow(bi, i), g)),
            pl.BlockSpec((ctx, HEAD_DIM), lambda bi, g, i: (nl // ctx + bi, g)),
            pl.BlockSpec((ctx, HEAD_DIM), lambda bi, g, i: (nl // ctx + bi, g)),
            pl.BlockSpec((seq, HEAD_DIM), lambda bi, g, i: (bi, g)),
            pl.BlockSpec((seq, HEAD_DIM), lambda bi, g, i: (bi, g)),
        ],
        out_specs=pl.BlockSpec((tq, rep * HEAD_DIM), lambda bi, g, i: (qrow(bi, i), g)),
        compiler_params=_cp(("arbitrary", "arbitrary", "arbitrary")),
        name="attention",
    )(qn, kn, vb, kn, vb)


def _dwconv_kernel(x_ref, p_ref, n_ref, w_ref, b_ref, o_ref, *, rows_lat, seq, ctx, act):
    tc = x_ref.shape[0]
    r0 = pl.program_id(0) * tc
    pos = jnp.where(r0 < rows_lat, r0 % seq, (r0 - rows_lat) % ctx)
    length = jnp.where(r0 < rows_lat, seq, ctx)
    has_prev = (pos > 0).astype(F32)
    has_next = (pos + tc < length).astype(F32)
    x = x_ref[...]
    row = lax.broadcasted_iota(I32, x.shape, 0)
    p1 = p_ref[SUBLANES - 1:SUBLANES, :] * has_prev
    p2 = p_ref[SUBLANES - 2:SUBLANES - 1, :] * has_prev
    n0 = n_ref[0:1, :] * has_next
    x1 = jnp.where(row == 0, p1, pltpu.roll(x, 1, 0))
    x2 = jnp.where(row == 0, p2, jnp.where(row == 1, p1, pltpu.roll(x, 2, 0)))
    xn = jnp.where(row == tc - 1, n0, pltpu.roll(x, tc - 1, 0))
    y = w_ref[0:1, :] * x2 + w_ref[1:2, :] * x1 + w_ref[2:3, :] * x + w_ref[3:4, :] * xn + b_ref[...]
    o_ref[...] = _silu(y) if act else y


def _dwconv(u, col_off, width, w, bias, dims, act):
    t = u.shape[0]
    tc = math.gcd(math.gcd(dims["L"], dims["C"]), TM_TOKEN)
    cb = col_off // width
    per = tc // SUBLANES
    last8 = t // SUBLANES - 1
    kernel = functools.partial(_dwconv_kernel, rows_lat=dims["NL"], seq=dims["L"], ctx=dims["C"], act=act)
    return pl.pallas_call(
        kernel,
        out_shape=jax.ShapeDtypeStruct((t, width), F32),
        grid=(t // tc,),
        in_specs=[
            pl.BlockSpec((tc, width), lambda i: (i, cb)),
            pl.BlockSpec((SUBLANES, width), lambda i: (jnp.maximum(i * per - 1, 0), cb)),
            pl.BlockSpec((SUBLANES, width), lambda i: (jnp.minimum((i + 1) * per, last8), cb)),
            pl.BlockSpec((CONV_W, width), lambda i: (0, 0)),
            pl.BlockSpec((1, width), lambda i: (0, 0)),
        ],
        out_specs=pl.BlockSpec((tc, width), lambda i: (i, 0)),
        compiler_params=_cp(("arbitrary",)),
        name="dwconv",
    )(u, u, u, w, bias.reshape(1, width))


def _lru_kernel(*refs, direction, final):
    if final:
        (u_ref, wa_ref, ba_ref, wx_ref, bx_ref, lam_ref, hb_ref, lg_ref, o_ref, h0_scr) = refs
    else:
        (u_ref, wa_ref, ba_ref, wx_ref, bx_ref, lam_ref, o_ref, h0_scr) = refs
    tc, width = u_ref.shape
    bw = width // LRU_BLOCKS

    @pl.when(pl.program_id(1) == 0)
    def _():
        h0_scr[...] = jnp.zeros_like(h0_scr)

    u = u_ref[...]
    ub = u.astype(BF16)
    ga = jnp.concatenate([_dot(ub[:, n * bw:(n + 1) * bw], wa_ref[n]) for n in range(LRU_BLOCKS)], axis=1)
    gx = jnp.concatenate([_dot(ub[:, n * bw:(n + 1) * bw], wx_ref[n]) for n in range(LRU_BLOCKS)], axis=1)
    r = _sigmoid(ga + ba_ref[...])
    i = _sigmoid(gx + bx_ref[...])
    log_a = (-LRU_C) * r * _softplus(-lam_ref[...])
    a = jnp.exp(log_a)
    bv = jnp.sqrt(1.0 - a * a) * (i * u)

    row = lax.broadcasted_iota(I32, a.shape, 0)
    s = 1
    while s < tc:
        if direction == 0:
            keep = row >= s
            a_s = jnp.where(keep, pltpu.roll(a, s, 0), 1.0)
            b_s = jnp.where(keep, pltpu.roll(bv, s, 0), 0.0)
        else:
            keep = row < tc - s
            a_s = jnp.where(keep, pltpu.roll(a, tc - s, 0), 1.0)
            b_s = jnp.where(keep, pltpu.roll(bv, tc - s, 0), 0.0)
        bv = a * b_s + bv
        a = a * a_s
        s *= 2
    h = bv + a * h0_scr[...]
    h0_scr[...] = h[tc - 1:tc, :] if direction == 0 else h[0:1, :]
    if final:
        o_ref[...] = (h + hb_ref[...]) * _gelu_tanh(lg_ref[...])
    else:
        o_ref[...] = h


def _lru_pass(uconv, u_all, lg_col_block, wa, ba, wx, bx, lam, hb, dims, direction):
    t, width = uconv.shape
    b, seq, ctx, nl = dims["B"], dims["L"], dims["C"], dims["NL"]
    tc = math.gcd(math.gcd(seq, ctx), TC_SCAN)
    n_c, n_l = ctx // tc, seq // tc
    final = hb is not None
    bw = width // LRU_BLOCKS

    def blk(bi, s):
        c_idx = s if direction == 0 else n_c - 1 - s
        l_idx = (s - n_c) if direction == 0 else n_l - 1 - (s - n_c)
        return jnp.where(s < n_c, nl // tc + bi * n_c + c_idx, bi * n_l + l_idx)

    row_spec = pl.BlockSpec((tc, width), lambda bi, s: (blk(bi, s), 0))
    vec = pl.BlockSpec((1, width), lambda bi, s: (0, 0))
    wspec = pl.BlockSpec((LRU_BLOCKS, bw, bw), lambda bi, s: (0, 0, 0))
    in_specs = [row_spec, wspec, vec, wspec, vec, vec]
    args = [uconv, wa, ba.reshape(1, width), wx, bx.reshape(1, width), lam.reshape(1, width)]
    if final:
        in_specs += [row_spec, pl.BlockSpec((tc, width), lambda bi, s: (blk(bi, s), lg_col_block))]
        args += [hb, u_all]
    kernel = functools.partial(_lru_kernel, direction=direction, final=final)
    return pl.pallas_call(
        kernel,
        out_shape=jax.ShapeDtypeStruct((t, width), F32),
        grid=(b, n_c + n_l),
        in_specs=in_specs,
        out_specs=row_spec,
        scratch_shapes=[pltpu.VMEM((1, width), F32)],
        compiler_params=_cp(("arbitrary", "arbitrary")),
        name="lru_dir%d" % direction,
    )(*args)


def _chdft_kernel(x_ref, c_ref, s_ref, z_ref):
    gw = c_ref.shape[0]
    n_g = x_ref.shape[1] // gw
    for g in range(n_g):
        xg = x_ref[:, g * gw:(g + 1) * gw].astype(BF16)
        z_ref[:, g * gw:(g + 1) * gw] = _dot(xg, c_ref[...]).astype(BF16)
        z_ref[:, (n_g + g) * gw:(n_g + g + 1) * gw] = _dot(xg, s_ref[...]).astype(BF16)


def _chdft(u, col_block, cmat, smat, dims):
    t = u.shape[0]
    tf = math.gcd(t, TM_TOKEN)
    return pl.pallas_call(
        _chdft_kernel,
        out_shape=jax.ShapeDtypeStruct((t, 2 * GROUP_W), BF16),
        grid=(t // tf,),
        in_specs=[
            pl.BlockSpec((tf, GROUP_W), lambda i: (i, col_block)),
            pl.BlockSpec(cmat.shape, lambda i: (0, 0)),
            pl.BlockSpec(smat.shape, lambda i: (0, 0)),
        ],
        out_specs=pl.BlockSpec((tf, 2 * GROUP_W), lambda i: (i, 0)),
        compiler_params=_cp(("arbitrary",)),
        name="chdft",
    )(u, cmat, smat)


def _seqdft_kernel(f_ref, zc_ref, zs_ref, *rest, scale):
    o_ref = rest[-1]
    n = zc_ref.shape[0]
    o_ref[...] = (_dot(f_ref[:, :n], zc_ref[...]) + _dot(f_ref[:, n:], zs_ref[...])) * scale


def _seqdft(fmat, z, prev_out, n_seq, seq_len, row_off, total_rows):
    tm = math.gcd(seq_len, TM_DFT)
    scale = 1.0 / math.sqrt(seq_len * (GROUP_W // FN_GROUPS))
    in_specs = [
        pl.BlockSpec((tm, 2 * seq_len), lambda b, i: (i, 0)),
        pl.BlockSpec((seq_len, GROUP_W), lambda b, i: (row_off // seq_len + b, 0)),
        pl.BlockSpec((seq_len, GROUP_W), lambda b, i: (row_off // seq_len + b, 1)),
    ]
    args = [fmat, z, z]
    aliases = {}
    if prev_out is not None:
        in_specs.append(pl.BlockSpec(memory_space=pl.ANY))
        args.append(prev_out)
        aliases = {3: 0}
    return pl.pallas_call(
        functools.partial(_seqdft_kernel, scale=scale),
        out_shape=jax.ShapeDtypeStruct((total_rows, GROUP_W), F32),
        grid=(n_seq, seq_len // tm),
        in_specs=in_specs,
        out_specs=pl.BlockSpec((tm, GROUP_W), lambda b, i: (row_off // tm + b * (seq_len // tm) + i, 0)),
        input_output_aliases=aliases,
        compiler_params=_cp(("arbitrary", "arbitrary")),
        name="seqdft",
    )(*args)


def _ssd_kernel(*refs, direction, final):
    if final:
        (xbc_ref, dt_ref, a_ref, bias_ref, e_ref, tri_ref, yf_ref, z_ref, dsk_ref, ng_ref,
         o_ref, st_scr) = refs
    else:
        (xbc_ref, dt_ref, a_ref, bias_ref, e_ref, tri_ref, o_ref, st_scr) = refs
    q = SSD_CHUNK
    gw = SSD_HEADS // SSD_GROUPS * SSD_HEAD_DIM
    hpg = SSD_HEADS // SSD_GROUPS
    n = SSD_STATE

    @pl.when(pl.program_id(1) == 0)
    def _():
        st_scr[...] = jnp.zeros_like(st_scr)

    e01 = e_ref[...]
    dtp = _softplus(dt_ref[...] + bias_ref[...])
    a = dtp * a_ref[...]
    acs = _dot_01x(tri_ref[...], a)
    acs_t = acs.T
    acs_e = _dot_x01(acs, e01)
    dt_e = _dot_x01(dtp, e01)
    edge_row = q - 1 if direction == 0 else 0
    edge = acs_e[edge_row:edge_row + 1, :]
    x = xbc_ref[:, :SSD_HEADS * SSD_HEAD_DIM]
    xdt = x * dt_e
    xdt_b = xdt.astype(BF16)
    w_b = (xdt * jnp.exp(edge - acs_e)).astype(BF16)
    e_acs = jnp.exp(acs_e)
    e_edge = jnp.exp(edge)

    li = lax.broadcasted_iota(I32, (q, q), 0)
    si = lax.broadcasted_iota(I32, (q, q), 1)
    causal = (li >= si) if direction == 0 else (li <= si)
    lane = lax.broadcasted_iota(I32, (q, 2 * SSD_HEAD_DIM), 1)
    first_half = lane < SSD_HEAD_DIM

    x_w = SSD_HEADS * SSD_HEAD_DIM
    ys = []
    for g in range(SSD_GROUPS):
        bg = xbc_ref[:, x_w + g * n:x_w + (g + 1) * n].astype(BF16)
        cg = xbc_ref[:, x_w + SSD_GROUPS * n + g * n:x_w + SSD_GROUPS * n + (g + 1) * n].astype(BF16)
        gmat = _dot_nt(cg, bg)
        s_in = st_scr[g]
        y_off = _dot(cg, s_in.astype(BF16)) * e_acs[:, g * gw:(g + 1) * gw]
        y_diag = []
        for pair in range(hpg // 2):
            ms = []
            for r in (2 * pair, 2 * pair + 1):
                ln = direction * SSD_HEADS + g * hpg + r
                seg = acs[:, ln:ln + 1] - acs_t[ln:ln + 1, :]
                ms.append((gmat * jnp.where(causal, jnp.exp(seg), 0.0)).astype(BF16))
            c0 = g * gw + pair * 2 * SSD_HEAD_DIM
            xp = xdt_b[:, c0:c0 + 2 * SSD_HEAD_DIM]
            zero = jnp.zeros_like(xp)
            rhs = jnp.concatenate([jnp.where(first_half, xp, zero), jnp.where(first_half, zero, xp)], axis=0)
            y_diag.append(_dot(jnp.concatenate(ms, axis=1), rhs))
        ys.append(jnp.concatenate(y_diag, axis=1) + y_off)
        st_new = _dot_tn(bg, w_b[:, g * gw:(g + 1) * gw])
        st_scr[g] = e_edge[:, g * gw:(g + 1) * gw] * s_in + st_new
    y = jnp.concatenate(ys, axis=1)
    if final:
        tot = x * dsk_ref[...] + yf_ref[...] + y
        o_ref[...] = _rms(tot * _silu(z_ref[...]), ng_ref[...])
    else:
        o_ref[...] = y


def _ssd_pass(xbc, dt, u_all, z_col_block, a_row, bias_row, e01, tri, yf, dsk, norm_g, dims, direction):
    t = xbc.shape[0]
    b, seq, ctx, nl = dims["B"], dims["L"], dims["C"], dims["NL"]
    q = SSD_CHUNK
    n_c, n_l = ctx // q, seq // q
    final = yf is not None
    width = SSD_HEADS * SSD_HEAD_DIM

    def blk(bi, s):
        c_idx = s if direction == 0 else n_c - 1 - s
        l_idx = (s - n_c) if direction == 0 else n_l - 1 - (s - n_c)
        return jnp.where(s < n_c, nl // q + bi * n_c + c_idx, bi * n_l + l_idx)

    def const(shape):
        return pl.BlockSpec(shape, lambda bi, s: (0,) * len(shape))

    out_spec = pl.BlockSpec((q, width), lambda bi, s: (blk(bi, s), 0))
    in_specs = [
        pl.BlockSpec((q, xbc.shape[1]), lambda bi, s: (blk(bi, s), 0)),
        pl.BlockSpec((q, LANES), lambda bi, s: (blk(bi, s), 0)),
        const((1, LANES)), const((1, LANES)), const(e01.shape), const(tri.shape),
    ]
    args = [xbc, dt, a_row, bias_row, e01, tri]
    if final:
        in_specs += [out_spec, pl.BlockSpec((q, width), lambda bi, s: (blk(bi, s), z_col_block)),
                     const((1, width)), const((1, width))]
        args += [yf, u_all, dsk, norm_g.reshape(1, width)]
    kernel = functools.partial(_ssd_kernel, direction=direction, final=final)
    return pl.pallas_call(
        kernel,
        out_shape=jax.ShapeDtypeStruct((t, width), F32),
        grid=(b, n_c + n_l),
        in_specs=in_specs,
        out_specs=out_spec,
        scratch_shapes=[pltpu.VMEM((SSD_GROUPS, SSD_STATE, width // SSD_GROUPS), F32)],
        compiler_params=_cp(("arbitrary", "arbitrary")),
        name="ssd_dir%d" % direction,
    )(*args)


def _outproj_kernel(att_ref, lru_ref, four_ref, ssd_ref, wo_ref, x_ref, mod_ref, n2_ref, rw_ref, rb_ref,
                    x1_ref, tok_ref, mi_ref, mg_ref, cnt_ref, carry):
    @pl.when(pl.program_id(0) == 0)
    def _():
        carry[...] = jnp.zeros_like(carry)

    feat = jnp.concatenate([att_ref[...].astype(BF16), lru_ref[...].astype(BF16),
                            four_ref[...].astype(BF16), ssd_ref[...].astype(BF16)], axis=1)
    acc = _dot(feat, wo_ref[...])
    x1 = x_ref[...] + mod_ref[2:3, :] * acc
    x1_ref[...] = x1
    tok = _rms(x1, n2_ref[...]) * (1.0 + mod_ref[4:5, :]) + mod_ref[3:4, :]
    tok_ref[...] = tok

    logits = _dot_hi(tok, rw_ref[...]) + rb_ref[...]
    tm = logits.shape[0]
    lane = lax.broadcasted_iota(I32, logits.shape, 1).astype(F32)
    work = logits
    vals, idxs = [], []
    for _ in range(TOP_K):
        m = jnp.max(work, axis=-1, keepdims=True)
        idx = jnp.min(jnp.where(work == m, lane, float(LANES)), axis=-1, keepdims=True)
        vals.append(m)
        idxs.append(idx)
        work = jnp.where(lane == idx, -jnp.inf, work)
    es = [jnp.exp(v - vals[0]) for v in vals]
    denom = es[0] + es[1] + es[2] + es[3]
    sel = jnp.zeros(logits.shape, F32)
    for idx in idxs:
        sel = sel + (lane == idx).astype(F32)
    ri = lax.broadcasted_iota(I32, (tm, tm), 0)
    ci = lax.broadcasted_iota(I32, (tm, tm), 1)
    strict_lower = (ci < ri).astype(BF16)
    cum = _dot(strict_lower, sel.astype(BF16)) + carry[...]
    carry[...] = carry[...] + jnp.sum(sel, axis=0, keepdims=True)
    cnt_ref[...] = jnp.broadcast_to(carry[...], cnt_ref.shape)
    mi = jnp.zeros(logits.shape, F32)
    mg = jnp.zeros(logits.shape, F32)
    for k in range(TOP_K):
        rank = jnp.sum(jnp.where(lane == idxs[k], cum, 0.0), axis=-1, keepdims=True)
        mi = jnp.where(lane == float(k), idxs[k], mi)
        mi = jnp.where(lane == float(TOP_K + k), rank, mi)
        mg = jnp.where(lane == float(k), es[k] / denom, mg)
    mi_ref[...] = mi.astype(I32)
    mg_ref[...] = mg


def _outproj(att, lru, four, ssd, wo4, x, mod6, norm2, rw, rb, dims):
    t, d = x.shape
    tm = math.gcd(math.gcd(dims["L"], dims["TC"]), TM_TOKEN)
    mod_row = dims["mod_row"]
    feat = pl.BlockSpec((tm, GROUP_W), lambda i: (i, 0))
    row_d = pl.BlockSpec((tm, d), lambda i: (i, 0))
    row_l = pl.BlockSpec((tm, LANES), lambda i: (i, 0))
    return pl.pallas_call(
        _outproj_kernel,
        out_shape=(jax.ShapeDtypeStruct((t, d), F32), jax.ShapeDtypeStruct((t, d), F32),
                   jax.ShapeDtypeStruct((t, LANES), I32), jax.ShapeDtypeStruct((t, LANES), F32),
                   jax.ShapeDtypeStruct((SUBLANES, LANES), F32)),
        grid=(t // tm,),
        in_specs=[feat, feat, feat, feat,
                  pl.BlockSpec(wo4.shape, lambda i: (0, 0)),
                  row_d,
                  pl.BlockSpec((None, 6, d), lambda i: (mod_row(i * tm), 0, 0)),
                  pl.BlockSpec((1, d), lambda i: (0, 0)),
                  pl.BlockSpec((d, LANES), lambda i: (0, 0)),
                  pl.BlockSpec((1, LANES), lambda i: (0, 0))],
        out_specs=(row_d, row_d, row_l, row_l, pl.BlockSpec((SUBLANES, LANES), lambda i: (0, 0))),
        scratch_shapes=[pltpu.VMEM((1, LANES), F32)],
        compiler_params=_cp(("arbitrary",)),
        name="outproj_router",
    )(att, lru, four, ssd, wo4, x, mod6, norm2.reshape(1, d), rw, rb)


def _gu_split_kernel(w_ref, p_ref, g_ref, u_ref):
    for c in range(g_ref.shape[1] // LANES):
        wb = w_ref[:, c * 2 * LANES:(c + 1) * 2 * LANES].astype(BF16)
        y = _dot(wb, p_ref[...])
        g_ref[:, c * LANES:(c + 1) * LANES] = y[:, :LANES].astype(BF16)
        u_ref[:, c * LANES:(c + 1) * LANES] = y[:, LANES:].astype(BF16)


def _gu_split(w_gu):
    g, d, ff2 = w_gu.shape
    ff = ff2 // 2
    tn = math.gcd(ff, 2 * LANES)
    perm = np.zeros((2 * LANES, 2 * LANES), np.float32)
    for j in range(LANES):
        perm[2 * j, j] = 1.0
        perm[2 * j + 1, LANES + j] = 1.0
    out = jax.ShapeDtypeStruct((g, d, ff), BF16)
    return pl.pallas_call(
        _gu_split_kernel,
        out_shape=(out, out),
        grid=(g, ff // tn),
        in_specs=[pl.BlockSpec((None, d, 2 * tn), lambda e, j: (e, 0, j)),
                  pl.BlockSpec((2 * LANES, 2 * LANES), lambda e, j: (0, 0))],
        out_specs=(pl.BlockSpec((None, d, tn), lambda e, j: (e, 0, j)),
                   pl.BlockSpec((None, d, tn), lambda e, j: (e, 0, j))),
        compiler_params=_cp(("arbitrary", "arbitrary")),
        name="gu_split",
    )(w_gu, jnp.asarray(perm, dtype=BF16))


ROWS_PER_ISSUE = 8


def _dispatch_kernel(zb_ref, dest_ref, tok_ref, xs_ref, zbuf, sem, zsem):
    td = tok_ref.shape[0]
    bm = zbuf.shape[0]

    @pl.when(pl.program_id(0) == 0)
    def _():
        zbuf[...] = jnp.zeros_like(zbuf)

        def zero_copy(z):
            start = pl.multiple_of(zb_ref[z] * bm, bm)
            return pltpu.make_async_copy(zbuf, xs_ref.at[pl.ds(start, bm)], zsem)

        def zstart(z, c):
            @pl.when(zb_ref[z] >= 0)
            def _():
                zero_copy(z).start()
            return c

        def zwait(z, c):
            @pl.when(zb_ref[z] >= 0)
            def _():
                zero_copy(z).wait()
            return c

        lax.fori_loop(0, zb_ref.shape[0], zstart, 0)
        lax.fori_loop(0, zb_ref.shape[0], zwait, 0)

    def issue(it, c):
        for rr in range(ROWS_PER_ISSUE):
            r = it * ROWS_PER_ISSUE + rr
            for k in range(TOP_K):
                pltpu.make_async_copy(tok_ref.at[pl.ds(r, 1)], xs_ref.at[pl.ds(dest_ref[r * TOP_K + k], 1)],
                                      sem).start(priority=k % 2)
        return c

    lax.fori_loop(0, td // ROWS_PER_ISSUE, issue, 0)
    for k in range(TOP_K):
        pltpu.make_async_copy(tok_ref, xs_ref.at[pl.ds(0, td)], sem).wait()


def _dispatch(tok, dest_flat, zero_blocks, p_rows):
    t, d = tok.shape
    td = math.gcd(t, 2 * TM_TOKEN)
    grid_spec = pltpu.PrefetchScalarGridSpec(
        num_scalar_prefetch=1,
        grid=(t // td,),
        in_specs=[
            pl.BlockSpec((td * TOP_K,), lambda i, zb: (i,), memory_space=pltpu.SMEM),
            pl.BlockSpec((td, d), lambda i, zb: (i, 0)),
        ],
        out_specs=pl.BlockSpec(memory_space=pl.ANY),
        scratch_shapes=[pltpu.VMEM((BM_MOE, d), F32), pltpu.SemaphoreType.DMA(()),
                        pltpu.SemaphoreType.DMA(())],
    )
    return pl.pallas_call(
        _dispatch_kernel,
        out_shape=jax.ShapeDtypeStruct((p_rows, d), F32),
        grid_spec=grid_spec,
        compiler_params=_cp(("arbitrary",)),
        name="moe_dispatch",
    )(zero_blocks, dest_flat, tok)


def _expert_kernel(be_ref, nu_ref, xs_ref, wg_ref, wu_ref, bg_ref, bu_ref, wd_ref, bd_ref, ys_ref):
    del be_ref
    i = pl.program_id(0)

    @pl.when(i < nu_ref[0])
    def _():
        x = xs_ref[...].astype(BF16)
        g = _dot(x, wg_ref[...]) + bg_ref[...]
        u = _dot(x, wu_ref[...]) + bu_ref[...]
        gate = jnp.minimum(g, SWIGLU_LIMIT)
        up = jnp.clip(u, -SWIGLU_LIMIT, SWIGLU_LIMIT)
        act = gate * _sigmoid(SWIGLU_ALPHA * gate) * (up + 1.0)
        ys_ref[...] = _dot(act.astype(BF16), wd_ref[...]) + bd_ref[...]

    @pl.when(i >= nu_ref[0])
    def _():
        ys_ref[...] = jnp.zeros_like(ys_ref)


def _experts(xs, blk_e, n_used, wg, wu, bg, bu, wd, bd):
    d = xs.shape[1]
    bm = BM_MOE
    p = blk_e.shape[0] * bm
    ff = wg.shape[2]

    def xrow(i, be, nu):
        return (jnp.minimum(i, nu[0] - 1), 0)

    def wsel(i, be, nu):
        return (be[i], 0, 0)

    grid_spec = pltpu.PrefetchScalarGridSpec(
        num_scalar_prefetch=2,
        grid=(p // bm,),
        in_specs=[
            pl.BlockSpec((bm, d), xrow),
            pl.BlockSpec((None, d, ff), wsel),
            pl.BlockSpec((None, d, ff), wsel),
            pl.BlockSpec((None, 1, ff), wsel),
            pl.BlockSpec((None, 1, ff), wsel),
            pl.BlockSpec((None, ff, d), wsel),
            pl.BlockSpec((None, 1, d), wsel),
        ],
        out_specs=pl.BlockSpec((bm, d), lambda i, be, nu: (i, 0)),
    )
    return pl.pallas_call(
        _expert_kernel,
        out_shape=jax.ShapeDtypeStruct((p, d), F32),
        grid_spec=grid_spec,
        compiler_params=_cp(("arbitrary",)),
        name="moe_experts",
    )(blk_e, n_used, xs, wg, wu, bg, bu, wd, bd)


def _combine_kernel(dcur_ref, dnext_ref, ys_ref, x1_ref, mg_ref, mod_ref, o_ref, buf, sem):
    i = pl.program_id(0)
    n = pl.num_programs(0)
    tm = x1_ref.shape[0]
    slot = i % 2

    def request(idx_ref, sl):
        def issue(it, c):
            for rr in range(ROWS_PER_ISSUE):
                r = it * ROWS_PER_ISSUE + rr
                for k in range(TOP_K):
                    pltpu.make_async_copy(ys_ref.at[pl.ds(idx_ref[r * TOP_K + k], 1)],
                                          buf.at[sl, k, pl.ds(r, 1)], sem.at[sl]).start(priority=k % 2)
            return c

        lax.fori_loop(0, tm // ROWS_PER_ISSUE, issue, 0)

    @pl.when(i == 0)
    def _():
        request(dcur_ref, 0)

    @pl.when(i + 1 < n)
    def _():
        request(dnext_ref, 1 - slot)

    for k in range(TOP_K):
        pltpu.make_async_copy(ys_ref.at[pl.ds(0, tm)], buf.at[slot, k], sem.at[slot]).wait()
    mg = mg_ref[...]
    ffn = mg[:, 0:1] * buf[slot, 0]
    for k in range(1, TOP_K):
        ffn = ffn + mg[:, k:k + 1] * buf[slot, k]
    o_ref[...] = x1_ref[...] + mod_ref[5:6, :] * ffn


def _combine(ys, dest_flat, x1, mg, mod6, dims):
    t, d = x1.shape
    tm = math.gcd(math.gcd(dims["L"], dims["TC"]), TM_TOKEN)
    mod_row = dims["mod_row"]
    nt = t // tm
    smem = functools.partial(pl.BlockSpec, memory_space=pltpu.SMEM)
    return pl.pallas_call(
        _combine_kernel,
        out_shape=jax.ShapeDtypeStruct((t, d), F32),
        grid=(nt,),
        in_specs=[
            smem((tm * TOP_K,), lambda i: (i,)),
            smem((tm * TOP_K,), lambda i: (jnp.minimum(i + 1, nt - 1),)),
            pl.BlockSpec(memory_space=pl.ANY),
            pl.BlockSpec((tm, d), lambda i: (i, 0)),
            pl.BlockSpec((tm, LANES), lambda i: (i, 0)),
            pl.BlockSpec((None, 6, d), lambda i: (mod_row(i * tm), 0, 0)),
        ],
        out_specs=pl.BlockSpec((tm, d), lambda i: (i, 0)),
        scratch_shapes=[pltpu.VMEM((2, TOP_K, tm, d), F32), pltpu.SemaphoreType.DMA((2,))],
        compiler_params=_cp(("arbitrary",)),
        name="moe_combine",
    )(dest_flat, dest_flat, ys, x1, mg, mod6)


def _rope_tables(seq, ctx):
    rows = seq // GRID_W
    row = np.repeat(np.arange(rows), GRID_W).astype(np.float32)
    col = np.tile(np.arange(GRID_W), rows).astype(np.float32)
    pairs = HEAD_DIM // 4
    inv = (ROPE_THETA ** (-np.arange(pairs, dtype=np.float32) / pairs)).astype(np.float32)
    ang = np.concatenate([row[:, None] * inv, col[:, None] * inv], axis=-1)
    ang = np.concatenate([ang, ang], axis=-1).astype(np.float32)
    sign = np.where(np.arange(HEAD_DIM) < HEAD_DIM // 2, -1.0, 1.0).astype(np.float32)
    cos = np.concatenate([np.ones((ctx, HEAD_DIM), np.float32), np.cos(ang)], axis=0)
    sin = np.concatenate([np.zeros((ctx, HEAD_DIM), np.float32), np.sin(ang) * sign], axis=0)
    return jnp.asarray(cos), jnp.asarray(sin)


def _dft_cos_sin(n):
    k = np.arange(n, dtype=np.int64)
    ang = 2.0 * np.pi * ((k[:, None] * k[None, :]) % n).astype(np.float64) / n
    return np.cos(ang), np.sin(ang)


def _seq_dft_matrix(n):
    c, s = _dft_cos_sin(n)
    return jnp.asarray(np.concatenate([c, -s], axis=1), dtype=BF16)


def _ssd_constants(direction):
    e = np.zeros((LANES, SSD_HEADS * SSD_HEAD_DIM), np.float32)
    for h in range(SSD_HEADS):
        e[direction * SSD_HEADS + h, h * SSD_HEAD_DIM:(h + 1) * SSD_HEAD_DIM] = 1.0
    li = np.arange(SSD_CHUNK)
    tri = (li[:, None] >= li[None, :]) if direction == 0 else (li[:, None] <= li[None, :])
    return jnp.asarray(e, dtype=BF16), jnp.asarray(tri.astype(np.float32), dtype=BF16)


def _lane_row(v):
    flat = v.reshape(1, -1).astype(F32)
    return jnp.pad(flat, ((0, 0), (0, LANES - flat.shape[1])))


def kernel(x, c, ctx, c_ctx, w_mod, b_mod, norm1, norm2, w_in, w_out, q_norm, k_norm, lru_conv_w, lru_conv_b,
           lru_wa, lru_ba, lru_wx, lru_bx, lru_lam, ssd_conv_w, ssd_conv_b, ssd_a_log, ssd_dt_bias, ssd_d,
           ssd_norm, router_w, router_b, exp_w_gu, exp_b_gu, exp_w_dn, exp_b_dn):
    b, seq, d = x.shape
    n_ctx = ctx.shape[1]
    depth = w_in.shape[0]
    n_exp = router_w.shape[2]
    ff = exp_w_dn.shape[2]
    nl, tc_rows = b * seq, b * n_ctx
    t = nl + tc_rows
    assert seq % SSD_CHUNK == 0 and n_ctx % SSD_CHUNK == 0 and seq % n_ctx == 0 and nl % n_ctx == 0
    assert seq % GRID_W == 0 and n_exp <= LANES and b + 1 <= 2 * SUBLANES

    def mod_row(r):
        return jnp.where(r < nl, r // seq, b)

    dims = dict(B=b, L=seq, C=n_ctx, NL=nl, TC=tc_rows, mod_row=mod_row)

    xa = jnp.concatenate([x.reshape(nl, d), ctx.reshape(tc_rows, d)], axis=0)
    cc = jnp.concatenate([c, c_ctx[None, :], jnp.zeros((2 * SUBLANES - b - 1, d), F32)], axis=0)
    mods = _mod_all(cc, w_mod, b_mod).reshape(depth, 2 * SUBLANES, 6, d)

    cos_t, sin_t = _rope_tables(seq, n_ctx)
    c128, s128 = _dft_cos_sin(GROUP_W // FN_GROUPS)
    c128, s128 = jnp.asarray(c128, dtype=BF16), jnp.asarray(s128, dtype=BF16)
    f_lat, f_ctx = _seq_dft_matrix(seq), _seq_dft_matrix(n_ctx)
    ssd_const = [_ssd_constants(0), _ssd_constants(1)]

    n_main = w_in.shape[2] - 2 * SSD_HEADS
    col_lx, col_lg, col_f, col_z, col_xbc = 1024, 1536, 2048, 2560, 3072
    xbc_w = n_main - col_xbc

    assert (t * TOP_K) % BM_MOE == 0
    n_blk = t * TOP_K // BM_MOE + n_exp
    wg_all, wu_all = _gu_split(exp_w_gu.reshape(depth * n_exp, d, 2 * ff))
    wg_all, wu_all = wg_all.reshape(depth, n_exp, d, ff), wu_all.reshape(depth, n_exp, d, ff)

    for l in range(depth):
        w_main = w_in[l, :, :n_main].astype(BF16)
        w_dt = jnp.pad(w_in[l, :, n_main:], ((0, 0), (0, LANES - 2 * SSD_HEADS))).astype(BF16)
        mod6 = mods[l]
        u, dt = _inproj(xa, mod6, norm1[l], w_main, w_dt, dims)

        qn, kn, vb = _qkprep(u, cos_t, sin_t, q_norm[l], k_norm[l], dims)
        att = _attention(qn, kn, vb, dims)

        lxc = _dwconv(u, col_lx, GROUP_W, lru_conv_w[l], lru_conv_b[l], dims, act=False)
        wa, wx = lru_wa[l].astype(BF16), lru_wx[l].astype(BF16)
        hb = _lru_pass(lxc, u, None, wa[1], lru_ba[l, 1], wx[1], lru_bx[l, 1], lru_lam[l, 1], None, dims, 1)
        lru = _lru_pass(lxc, u, col_lg // GROUP_W, wa[0], lru_ba[l, 0], wx[0], lru_bx[l, 0], lru_lam[l, 0],
                        hb, dims, 0)

        z = _chdft(u, col_f // GROUP_W, c128, s128, dims)
        four = _seqdft(f_lat, z, jnp.zeros((t, GROUP_W), F32), b, seq, 0, t)
        four = _seqdft(f_ctx, z, four, b, n_ctx, nl, t)

        xbc = _dwconv(u, col_xbc, xbc_w, ssd_conv_w[l], ssd_conv_b[l], dims, act=True)
        a_row = _lane_row(-jnp.exp(ssd_a_log[l]))
        bias_row = _lane_row(ssd_dt_bias[l])
        dsk = jnp.repeat(ssd_d[l], SSD_HEAD_DIM).reshape(1, -1)
        yf = _ssd_pass(xbc, dt, u, None, a_row, bias_row, *ssd_const[0], None, None, None, dims, 0)
        ssd = _ssd_pass(xbc, dt, u, col_z // GROUP_W, a_row, bias_row, *ssd_const[1], yf, dsk, ssd_norm[l],
                        dims, 1)

        wo4 = w_out[l].astype(BF16)
        rw = jnp.pad(router_w[l], ((0, 0), (0, LANES - n_exp)))
        rb = jnp.pad(router_b[l], (0, LANES - n_exp), constant_values=NEG_BIG).reshape(1, LANES)
        x1, tok, mi, mg, cnt = _outproj(att, lru, four, ssd, wo4, xa, mod6, norm2[l], rw, rb, dims)

        counts = cnt[0, :n_exp].astype(I32)
        pcounts = (counts + BM_MOE - 1) // BM_MOE * BM_MOE
        pends = jnp.cumsum(pcounts)
        pstarts = pends - pcounts
        dest = (pstarts[mi[:, :TOP_K]] + mi[:, TOP_K:2 * TOP_K]).reshape(-1)
        blk_start = jnp.arange(n_blk, dtype=I32) * BM_MOE
        blk_e = jnp.minimum(jnp.sum((pends[None, :] <= blk_start[:, None]).astype(I32), axis=1), n_exp - 1)
        n_used = (pends[-1:] // BM_MOE).astype(I32)
        trail = n_used[0] + jnp.arange(n_exp, dtype=I32)
        zero_blocks = jnp.concatenate([
            jnp.where(pcounts > 0, pends // BM_MOE - 1, -1),
            jnp.where(trail < n_blk, trail, -1)]).astype(I32)

        xs = _dispatch(tok, dest, zero_blocks, n_blk * BM_MOE)
        bgu = exp_b_gu[l].reshape(n_exp, 1, ff, 2)
        ys = _experts(xs, blk_e, n_used, wg_all[l], wu_all[l], bgu[..., 0], bgu[..., 1],
                      exp_w_dn[l].astype(BF16), exp_b_dn[l].reshape(n_exp, 1, d))
        xa = _combine(ys, dest, x1, mg, mod6, dims)

    return xa[:nl].reshape(b, seq, d)
```

```python
import functools
import math

import jax
import jax.numpy as jnp
import numpy as np
from jax import lax
from jax.experimental import pallas as pl
from jax.experimental.pallas import tpu as pltpu

F32 = jnp.float32
BF16 = jnp.bfloat16
I32 = jnp.int32

HEAD_DIM = 128
ATT_Q_HEADS = 4
ATT_KV_HEADS = 2
GRID_W = 64
ROPE_THETA = 10000.0
CONV_W = 4
LRU_BLOCKS = 4
LRU_C = 8.0
FN_GROUPS = 4
SSD_HEAD_DIM = 64
SSD_HEADS = 8
SSD_GROUPS = 2
SSD_STATE = 128
SSD_CHUNK = 128
TOP_K = 4
SWIGLU_LIMIT = 7.0
SWIGLU_ALPHA = 1.702
EPS = 1e-6
GROUP_W = 512

LANES = 128
SUBLANES = 8
VMEM_LIMIT = 56 * 1024 * 1024

NEG_BIG = -1e30

TM_INPROJ = 1024
TN_INPROJ = 512
TM_TOKEN = 256
TQ_ATT = 256
TK_ATT = 2048
TC_SCAN = 256
TM_DFT = 512
BM_MOE = 512


def _cp(sem, vmem=VMEM_LIMIT):
    return pltpu.CompilerParams(dimension_semantics=sem, vmem_limit_bytes=vmem)


def _dot(a, b):
    return jnp.dot(a, b, preferred_element_type=F32)


def _dot_nt(a, b):
    return lax.dot_general(a, b, (((1,), (1,)), ((), ())), preferred_element_type=F32)


def _dot_tn(a, b):
    return lax.dot_general(a, b, (((0,), (0,)), ((), ())), preferred_element_type=F32)


def _split3(x):
    hi = x.astype(BF16)
    r = x - hi.astype(F32)
    mid = r.astype(BF16)
    lo = (r - mid.astype(F32)).astype(BF16)
    return hi, mid, lo


def _dot_x01(x, m01):
    hi, mid, lo = _split3(x)
    return _dot(hi, m01) + _dot(mid, m01) + _dot(lo, m01)


def _dot_01x(m01, x):
    hi, mid, lo = _split3(x)
    return _dot(m01, hi) + _dot(m01, mid) + _dot(m01, lo)


def _dot_hi(a, b):
    ah = a.astype(BF16)
    al = (a - ah.astype(F32)).astype(BF16)
    bh = b.astype(BF16)
    bl = (b - bh.astype(F32)).astype(BF16)
    return _dot(ah, bh) + _dot(ah, bl) + _dot(al, bh)


def _sigmoid(x):
    return 1.0 / (1.0 + jnp.exp(-x))


def _silu(x):
    return x * _sigmoid(x)


def _softplus(x):
    return jnp.maximum(x, 0.0) + jnp.log(1.0 + jnp.exp(-jnp.abs(x)))


def _gelu_tanh(x):
    c = math.sqrt(2.0 / math.pi)
    return 0.5 * x * (1.0 + jnp.tanh(c * (x + 0.044715 * (x * x * x))))


def _rms(x, g):
    return x * lax.rsqrt(jnp.mean(x * x, axis=-1, keepdims=True) + EPS) * g


def _mod_kernel(cc_ref, w_ref, b_ref, o_ref):
    s = _silu(cc_ref[...])
    o_ref[...] = _dot_hi(s, w_ref[...]) + b_ref[...]


def _mod_all(cc, w_mod, b_mod):
    n_layers, d, d6 = w_mod.shape
    tn = 1024
    rows = cc.shape[0]
    return pl.pallas_call(
        _mod_kernel,
        out_shape=jax.ShapeDtypeStruct((n_layers, rows, d6), F32),
        grid=(n_layers, d6 // tn),
        in_specs=[
            pl.BlockSpec((rows, d), lambda l, j: (0, 0)),
            pl.BlockSpec((None, d, tn), lambda l, j: (l, 0, j)),
            pl.BlockSpec((None, 1, tn), lambda l, j: (l, 0, j)),
        ],
        out_specs=pl.BlockSpec((None, rows, tn), lambda l, j: (l, 0, j)),
        compiler_params=_cp(("arbitrary", "arbitrary")),
        name="mod_all",
    )(cc, w_mod, b_mod.reshape(n_layers, 1, d6))


def _inproj_kernel(x_ref, mod_ref, n1_ref, w_ref, wdt_ref, u_ref, dt_ref, h_scr):
    @pl.when(pl.program_id(1) == 0)
    def _():
        x = x_ref[...]
        h = _rms(x, n1_ref[...]) * (1.0 + mod_ref[1:2, :]) + mod_ref[0:1, :]
        hb = h.astype(BF16)
        h_scr[...] = hb
        dt_ref[...] = _dot(hb, wdt_ref[...])

    u_ref[...] = _dot(h_scr[...], w_ref[...])


def _inproj(x, mod6, norm1, w_main, w_dt, dims):
    t, d = x.shape
    n_main = w_main.shape[1]
    tm = math.gcd(math.gcd(dims["L"], dims["TC"]), TM_INPROJ)
    tn = TN_INPROJ
    mod_row = dims["mod_row"]
    return pl.pallas_call(
        _inproj_kernel,
        out_shape=(jax.ShapeDtypeStruct((t, n_main), F32), jax.ShapeDtypeStruct((t, LANES), F32)),
        grid=(t // tm, n_main // tn),
        in_specs=[
            pl.BlockSpec((tm, d), lambda i, j: (i, 0)),
            pl.BlockSpec((None, 6, d), lambda i, j: (mod_row(i * tm), 0, 0)),
            pl.BlockSpec((1, d), lambda i, j: (0, 0)),
            pl.BlockSpec((d, tn), lambda i, j: (0, j)),
            pl.BlockSpec((d, LANES), lambda i, j: (0, 0)),
        ],
        out_specs=(pl.BlockSpec((tm, tn), lambda i, j: (i, j)),
                   pl.BlockSpec((tm, LANES), lambda i, j: (i, 0))),
        scratch_shapes=[pltpu.VMEM((tm, d), BF16)],
        compiler_params=_cp(("arbitrary", "arbitrary")),
        name="inproj",
    )(x, mod6, norm1.reshape(1, d), w_main, w_dt)


def _qkprep_kernel(q_ref, k_ref, v_ref, cos_ref, sin_ref, qn_ref, kn_ref, qo_ref, ko_ref, vo_ref):
    cos = cos_ref[...]
    sin = sin_ref[...]

    def prep(xh, g):
        y = _rms(xh, g)
        return y * cos + pltpu.roll(y, HEAD_DIM // 2, 1) * sin

    for h in range(ATT_Q_HEADS):
        sl = slice(h * HEAD_DIM, (h + 1) * HEAD_DIM)
        qo_ref[:, sl] = prep(q_ref[:, sl], qn_ref[...]).astype(BF16)
    for h in range(ATT_KV_HEADS):
        sl = slice(h * HEAD_DIM, (h + 1) * HEAD_DIM)
        ko_ref[:, sl] = prep(k_ref[:, sl], kn_ref[...]).astype(BF16)
    vo_ref[...] = v_ref[...].astype(BF16)


def _qkprep(u, cos_t, sin_t, q_norm, k_norm, dims):
    t = u.shape[0]
    tp = math.gcd(math.gcd(dims["L"], dims["C"]), TM_TOKEN)
    nl, seq, ctx = dims["NL"], dims["L"], dims["C"]

    def tab(i):
        r = i * tp
        return jnp.where(r < nl, ctx // tp + (r % seq) // tp, 0)

    qw, kw = ATT_Q_HEADS * HEAD_DIM, ATT_KV_HEADS * HEAD_DIM
    return pl.pallas_call(
        _qkprep_kernel,
        out_shape=(jax.ShapeDtypeStruct((t, qw), BF16), jax.ShapeDtypeStruct((t, kw), BF16),
                   jax.ShapeDtypeStruct((t, kw), BF16)),
        grid=(t // tp,),
        in_specs=[
            pl.BlockSpec((tp, qw), lambda i: (i, 0)),
            pl.BlockSpec((tp, kw), lambda i: (i, qw // kw)),
            pl.BlockSpec((tp, kw), lambda i: (i, qw // kw + 1)),
            pl.BlockSpec((tp, HEAD_DIM), lambda i: (tab(i), 0)),
            pl.BlockSpec((tp, HEAD_DIM), lambda i: (tab(i), 0)),
            pl.BlockSpec((1, HEAD_DIM), lambda i: (0, 0)),
            pl.BlockSpec((1, HEAD_DIM), lambda i: (0, 0)),
        ],
        out_specs=(pl.BlockSpec((tp, qw), lambda i: (i, 0)),
                   pl.BlockSpec((tp, kw), lambda i: (i, 0)),
                   pl.BlockSpec((tp, kw), lambda i: (i, 0))),
        compiler_params=_cp(("arbitrary",)),
        name="qkprep",
    )(u, u, u, cos_t, sin_t, q_norm.reshape(1, HEAD_DIM), k_norm.reshape(1, HEAD_DIM))


def _attn_kernel(q_ref, kc_ref, vc_ref, kl_ref, vl_ref, o_ref, *, n_lat_q, n_lat_chunks, tk):
    tq = q_ref.shape[0]
    scale = HEAD_DIM ** -0.5
    q2 = jnp.concatenate([q_ref[:, :HEAD_DIM], q_ref[:, HEAD_DIM:]], axis=0)

    def step(k, v, carry):
        m, l, acc = carry
        s = _dot_nt(q2, k) * scale
        m_new = jnp.maximum(m, jnp.max(s, axis=-1, keepdims=True))
        alpha = jnp.exp(m - m_new)
        p = jnp.exp(s - m_new)
        l = alpha * l + jnp.sum(p, axis=-1, keepdims=True)
        acc = alpha * acc + _dot(p.astype(BF16), v)
        return m_new, l, acc

    init = (jnp.full((2 * tq, 1), NEG_BIG, F32), jnp.zeros((2 * tq, 1), F32),
            jnp.zeros((2 * tq, HEAD_DIM), F32))
    carry = step(kc_ref[...], vc_ref[...], init)

    def body(j, c):
        off = pl.multiple_of(j * tk, tk)
        return step(kl_ref[pl.ds(off, tk), :], vl_ref[pl.ds(off, tk), :], c)

    n = jnp.where(pl.program_id(2) < n_lat_q, n_lat_chunks, 0)
    m, l, acc = lax.fori_loop(0, n, body, carry)
    o = acc / l
    o_ref[:, :HEAD_DIM] = o[:tq]
    o_ref[:, HEAD_DIM:] = o[tq:]


def _attention(qn, kn, vb, dims):
    t = qn.shape[0]
    b, seq, ctx, nl = dims["B"], dims["L"], dims["C"], dims["NL"]
    tq = math.gcd(math.gcd(seq, ctx), TQ_ATT)
    tk = math.gcd(seq, TK_ATT)
    n_lat_q, n_ctx_q = seq // tq, ctx // tq
    rep = ATT_Q_HEADS // ATT_KV_HEADS

    def qrow(bi, i):
        return jnp.where(i < n_lat_q, bi * n_lat_q + i, nl // tq + bi * n_ctx_q + (i - n_lat_q))

    kernel = functools.partial(_attn_kernel, n_lat_q=n_lat_q, n_lat_chunks=seq // tk, tk=tk)
    return pl.pallas_call(
        kernel,
        out_shape=jax.ShapeDtypeStruct((t, ATT_Q_HEADS * HEAD_DIM), F32),
        grid=(b, ATT_KV_HEADS, n_lat_q + n_ctx_q),
        in_specs=[
            pl.BlockSpec((tq, rep * HEAD_DIM), lambda bi, g, i: (qrow(bi, i), g)),
            pl.BlockSpec((ctx, HEAD_DIM), lambda bi, g, i: (nl // ctx + bi, g)),
            pl.BlockSpec((ctx, HEAD_DIM), lambda bi, g, i: (nl // ctx + bi, g)),
            pl.BlockSpec((seq, HEAD_DIM), lambda bi, g, i: (bi, g)),
            pl.BlockSpec((seq, HEAD_DIM), lambda bi, g, i: (bi, g)),
        ],
        out_specs=pl.BlockSpec((tq, rep * HEAD_DIM), lambda bi, g, i: (qrow(bi, i), g)),
        compiler_params=_cp(("arbitrary", "arbitrary", "arbitrary")),
        name="attention",
    )(qn, kn, vb, kn, vb)


def _dwconv_kernel(x_ref, p_ref, n_ref, w_ref, b_ref, o_ref, *, rows_lat, seq, ctx, act):
    tc = x_ref.shape[0]
    r0 = pl.program_id(0) * tc
    pos = jnp.where(r0 < rows_lat, r0 % seq, (r0 - rows_lat) % ctx)
    length = jnp.where(r0 < rows_lat, seq, ctx)
    has_prev = (pos > 0).astype(F32)
    has_next = (pos + tc < length).astype(F32)
    x = x_ref[...]
    row = lax.broadcasted_iota(I32, x.shape, 0)
    p1 = p_ref[SUBLANES - 1:SUBLANES, :] * has_prev
    p2 = p_ref[SUBLANES - 2:SUBLANES - 1, :] * has_prev
    n0 = n_ref[0:1, :] * has_next
    x1 = jnp.where(row == 0, p1, pltpu.roll(x, 1, 0))
    x2 = jnp.where(row == 0, p2, jnp.where(row == 1, p1, pltpu.roll(x, 2, 0)))
    xn = jnp.where(row == tc - 1, n0, pltpu.roll(x, tc - 1, 0))
    y = w_ref[0:1, :] * x2 + w_ref[1:2, :] * x1 + w_ref[2:3, :] * x + w_ref[3:4, :] * xn + b_ref[...]
    o_ref[...] = _silu(y) if act else y


def _dwconv(u, col_off, width, w, bias, dims, act):
    t = u.shape[0]
    tc = math.gcd(math.gcd(dims["L"], dims["C"]), TM_TOKEN)
    cb = col_off // width
    per = tc // SUBLANES
    last8 = t // SUBLANES - 1
    kernel = functools.partial(_dwconv_kernel, rows_lat=dims["NL"], seq=dims["L"], ctx=dims["C"], act=act)
    return pl.pallas_call(
        kernel,
        out_shape=jax.ShapeDtypeStruct((t, width), F32),
        grid=(t // tc,),
        in_specs=[
            pl.BlockSpec((tc, width), lambda i: (i, cb)),
            pl.BlockSpec((SUBLANES, width), lambda i: (jnp.maximum(i * per - 1, 0), cb)),
            pl.BlockSpec((SUBLANES, width), lambda i: (jnp.minimum((i + 1) * per, last8), cb)),
            pl.BlockSpec((CONV_W, width), lambda i: (0, 0)),
            pl.BlockSpec((1, width), lambda i: (0, 0)),
        ],
        out_specs=pl.BlockSpec((tc, width), lambda i: (i, 0)),
        compiler_params=_cp(("arbitrary",)),
        name="dwconv",
    )(u, u, u, w, bias.reshape(1, width))


def _lru_kernel(*refs, direction, final):
    if final:
        (u_ref, wa_ref, ba_ref, wx_ref, bx_ref, lam_ref, hb_ref, lg_ref, o_ref, h0_scr) = refs
    else:
        (u_ref, wa_ref, ba_ref, wx_ref, bx_ref, lam_ref, o_ref, h0_scr) = refs
    tc, width = u_ref.shape
    bw = width // LRU_BLOCKS

    @pl.when(pl.program_id(1) == 0)
    def _():
        h0_scr[...] = jnp.zeros_like(h0_scr)

    u = u_ref[...]
    ub = u.astype(BF16)
    ga = jnp.concatenate([_dot(ub[:, n * bw:(n + 1) * bw], wa_ref[n]) for n in range(LRU_BLOCKS)], axis=1)
    gx = jnp.concatenate([_dot(ub[:, n * bw:(n + 1) * bw], wx_ref[n]) for n in range(LRU_BLOCKS)], axis=1)
    r = _sigmoid(ga + ba_ref[...])
    i = _sigmoid(gx + bx_ref[...])
    log_a = (-LRU_C) * r * _softplus(-lam_ref[...])
    a = jnp.exp(log_a)
    bv = jnp.sqrt(1.0 - a * a) * (i * u)

    row = lax.broadcasted_iota(I32, a.shape, 0)
    s = 1
    while s < tc:
        if direction == 0:
            keep = row >= s
            a_s = jnp.where(keep, pltpu.roll(a, s, 0), 1.0)
            b_s = jnp.where(keep, pltpu.roll(bv, s, 0), 0.0)
        else:
            keep = row < tc - s
            a_s = jnp.where(keep, pltpu.roll(a, tc - s, 0), 1.0)
            b_s = jnp.where(keep, pltpu.roll(bv, tc - s, 0), 0.0)
        bv = a * b_s + bv
        a = a * a_s
        s *= 2
    h = bv + a * h0_scr[...]
    h0_scr[...] = h[tc - 1:tc, :] if direction == 0 else h[0:1, :]
    if final:
        o_ref[...] = (h + hb_ref[...]) * _gelu_tanh(lg_ref[...])
    else:
        o_ref[...] = h


def _lru_pass(uconv, u_all, lg_col_block, wa, ba, wx, bx, lam, hb, dims, direction):
    t, width = uconv.shape
    b, seq, ctx, nl = dims["B"], dims["L"], dims["C"], dims["NL"]
    tc = math.gcd(math.gcd(seq, ctx), TC_SCAN)
    n_c, n_l = ctx // tc, seq // tc
    final = hb is not None
    bw = width // LRU_BLOCKS

    def blk(bi, s):
        c_idx = s if direction == 0 else n_c - 1 - s
        l_idx = (s - n_c) if direction == 0 else n_l - 1 - (s - n_c)
        return jnp.where(s < n_c, nl // tc + bi * n_c + c_idx, bi * n_l + l_idx)

    row_spec = pl.BlockSpec((tc, width), lambda bi, s: (blk(bi, s), 0))
    vec = pl.BlockSpec((1, width), lambda bi, s: (0, 0))
    wspec = pl.BlockSpec((LRU_BLOCKS, bw, bw), lambda bi, s: (0, 0, 0))
    in_specs = [row_spec, wspec, vec, wspec, vec, vec]
    args = [uconv, wa, ba.reshape(1, width), wx, bx.reshape(1, width), lam.reshape(1, width)]
    if final:
        in_specs += [row_spec, pl.BlockSpec((tc, width), lambda bi, s: (blk(bi, s), lg_col_block))]
        args += [hb, u_all]
    kernel = functools.partial(_lru_kernel, direction=direction, final=final)
    return pl.pallas_call(
        kernel,
        out_shape=jax.ShapeDtypeStruct((t, width), F32),
        grid=(b, n_c + n_l),
        in_specs=in_specs,
        out_specs=row_spec,
        scratch_shapes=[pltpu.VMEM((1, width), F32)],
        compiler_params=_cp(("arbitrary", "arbitrary")),
        name="lru_dir%d" % direction,
    )(*args)


def _chdft_kernel(x_ref, c_ref, s_ref, z_ref):
    gw = c_ref.shape[0]
    n_g = x_ref.shape[1] // gw
    for g in range(n_g):
        xg = x_ref[:, g * gw:(g + 1) * gw].astype(BF16)
        z_ref[:, g * gw:(g + 1) * gw] = _dot(xg, c_ref[...]).astype(BF16)
        z_ref[:, (n_g + g) * gw:(n_g + g + 1) * gw] = _dot(xg, s_ref[...]).astype(BF16)


def _chdft(u, col_block, cmat, smat, dims):
    t = u.shape[0]
    tf = math.gcd(t, TM_TOKEN)
    return pl.pallas_call(
        _chdft_kernel,
        out_shape=jax.ShapeDtypeStruct((t, 2 * GROUP_W), BF16),
        grid=(t // tf,),
        in_specs=[
            pl.BlockSpec((tf, GROUP_W), lambda i: (i, col_block)),
            pl.BlockSpec(cmat.shape, lambda i: (0, 0)),
            pl.BlockSpec(smat.shape, lambda i: (0, 0)),
        ],
        out_specs=pl.BlockSpec((tf, 2 * GROUP_W), lambda i: (i, 0)),
        compiler_params=_cp(("arbitrary",)),
        name="chdft",
    )(u, cmat, smat)


def _seqdft_kernel(f_ref, zc_ref, zs_ref, *rest, scale):
    o_ref = rest[-1]
    n = zc_ref.shape[0]
    o_ref[...] = (_dot(f_ref[:, :n], zc_ref[...]) + _dot(f_ref[:, n:], zs_ref[...])) * scale


def _seqdft(fmat, z, prev_out, n_seq, seq_len, row_off, total_rows):
    tm = math.gcd(seq_len, TM_DFT)
    scale = 1.0 / math.sqrt(seq_len * (GROUP_W // FN_GROUPS))
    in_specs = [
        pl.BlockSpec((tm, 2 * seq_len), lambda b, i: (i, 0)),
        pl.BlockSpec((seq_len, GROUP_W), lambda b, i: (row_off // seq_len + b, 0)),
        pl.BlockSpec((seq_len, GROUP_W), lambda b, i: (row_off // seq_len + b, 1)),
    ]
    args = [fmat, z, z]
    aliases = {}
    if prev_out is not None:
        in_specs.append(pl.BlockSpec(memory_space=pl.ANY))
        args.append(prev_out)
        aliases = {3: 0}
    return pl.pallas_call(
        functools.partial(_seqdft_kernel, scale=scale),
        out_shape=jax.ShapeDtypeStruct((total_rows, GROUP_W), F32),
        grid=(n_seq, seq_len // tm),
        in_specs=in_specs,
        out_specs=pl.BlockSpec((tm, GROUP_W), lambda b, i: (row_off // tm + b * (seq_len // tm) + i, 0)),
        input_output_aliases=aliases,
        compiler_params=_cp(("arbitrary", "arbitrary")),
        name="seqdft",
    )(*args)


def _ssd_kernel(*refs, direction, final):
    if final:
        (xbc_ref, dt_ref, a_ref, bias_ref, e_ref, tri_ref, yf_ref, z_ref, dsk_ref, ng_ref,
         o_ref, st_scr) = refs
    else:
        (xbc_ref, dt_ref, a_ref, bias_ref, e_ref, tri_ref, o_ref, st_scr) = refs
    q = SSD_CHUNK
    gw = SSD_HEADS // SSD_GROUPS * SSD_HEAD_DIM
    hpg = SSD_HEADS // SSD_GROUPS
    n = SSD_STATE

    @pl.when(pl.program_id(1) == 0)
    def _():
        st_scr[...] = jnp.zeros_like(st_scr)

    e01 = e_ref[...]
    dtp = _softplus(dt_ref[...] + bias_ref[...])
    a = dtp * a_ref[...]
    acs = _dot_01x(tri_ref[...], a)
    acs_t = acs.T
    acs_e = _dot_x01(acs, e01)
    dt_e = _dot_x01(dtp, e01)
    edge_row = q - 1 if direction == 0 else 0
    edge = acs_e[edge_row:edge_row + 1, :]
    x = xbc_ref[:, :SSD_HEADS * SSD_HEAD_DIM]
    xdt = x * dt_e
    xdt_b = xdt.astype(BF16)
    w_b = (xdt * jnp.exp(edge - acs_e)).astype(BF16)
    e_acs = jnp.exp(acs_e)
    e_edge = jnp.exp(edge)

    li = lax.broadcasted_iota(I32, (q, q), 0)
    si = lax.broadcasted_iota(I32, (q, q), 1)
    causal = (li >= si) if direction == 0 else (li <= si)
    lane = lax.broadcasted_iota(I32, (q, 2 * SSD_HEAD_DIM), 1)
    first_half = lane < SSD_HEAD_DIM

    x_w = SSD_HEADS * SSD_HEAD_DIM
    ys = []
    for g in range(SSD_GROUPS):
        bg = xbc_ref[:, x_w + g * n:x_w + (g + 1) * n].astype(BF16)
        cg = xbc_ref[:, x_w + SSD_GROUPS * n + g * n:x_w + SSD_GROUPS * n + (g + 1) * n].astype(BF16)
        gmat = _dot_nt(cg, bg)
        s_in = st_scr[g]
        y_off = _dot(cg, s_in.astype(BF16)) * e_acs[:, g * gw:(g + 1) * gw]
        y_diag = []
        for pair in range(hpg // 2):
            ms = []
            for r in (2 * pair, 2 * pair + 1):
                ln = direction * SSD_HEADS + g * hpg + r
                seg = acs[:, ln:ln + 1] - acs_t[ln:ln + 1, :]
                ms.append((gmat * jnp.where(causal, jnp.exp(seg), 0.0)).astype(BF16))
            c0 = g * gw + pair * 2 * SSD_HEAD_DIM
            xp = xdt_b[:, c0:c0 + 2 * SSD_HEAD_DIM]
            zero = jnp.zeros_like(xp)
            rhs = jnp.concatenate([jnp.where(first_half, xp, zero), jnp.where(first_half, zero, xp)], axis=0)
            y_diag.append(_dot(jnp.concatenate(ms, axis=1), rhs))
        ys.append(jnp.concatenate(y_diag, axis=1) + y_off)
        st_new = _dot_tn(bg, w_b[:, g * gw:(g + 1) * gw])
        st_scr[g] = e_edge[:, g * gw:(g + 1) * gw] * s_in + st_new
    y = jnp.concatenate(ys, axis=1)
    if final:
        tot = x * dsk_ref[...] + yf_ref[...] + y
        o_ref[...] = _rms(tot * _silu(z_ref[...]), ng_ref[...])
    else:
        o_ref[...] = y


def _ssd_pass(xbc, dt, u_all, z_col_block, a_row, bias_row, e01, tri, yf, dsk, norm_g, dims, direction):
    t = xbc.shape[0]
    b, seq, ctx, nl = dims["B"], dims["L"], dims["C"], dims["NL"]
    q = SSD_CHUNK
    n_c, n_l = ctx // q, seq // q
    final = yf is not None
    width = SSD_HEADS * SSD_HEAD_DIM

    def blk(bi, s):
        c_idx = s if direction == 0 else n_c - 1 - s
        l_idx = (s - n_c) if direction == 0 else n_l - 1 - (s - n_c)
        return jnp.where(s < n_c, nl // q + bi * n_c + c_idx, bi * n_l + l_idx)

    def const(shape):
        return pl.BlockSpec(shape, lambda bi, s: (0,) * len(shape))

    out_spec = pl.BlockSpec((q, width), lambda bi, s: (blk(bi, s), 0))
    in_specs = [
        pl.BlockSpec((q, xbc.shape[1]), lambda bi, s: (blk(bi, s), 0)),
        pl.BlockSpec((q, LANES), lambda bi, s: (blk(bi, s), 0)),
        const((1, LANES)), const((1, LANES)), const(e01.shape), const(tri.shape),
    ]
    args = [xbc, dt, a_row, bias_row, e01, tri]
    if final:
        in_specs += [out_spec, pl.BlockSpec((q, width), lambda bi, s: (blk(bi, s), z_col_block)),
                     const((1, width)), const((1, width))]
        args += [yf, u_all, dsk, norm_g.reshape(1, width)]
    kernel = functools.partial(_ssd_kernel, direction=direction, final=final)
    return pl.pallas_call(
        kernel,
        out_shape=jax.ShapeDtypeStruct((t, width), F32),
        grid=(b, n_c + n_l),
        in_specs=in_specs,
        out_specs=out_spec,
        scratch_shapes=[pltpu.VMEM((SSD_GROUPS, SSD_STATE, width // SSD_GROUPS), F32)],
        compiler_params=_cp(("arbitrary", "arbitrary")),
        name="ssd_dir%d" % direction,
    )(*args)


def _outproj_kernel(att_ref, lru_ref, four_ref, ssd_ref, wo_ref, x_ref, mod_ref, n2_ref, rw_ref, rb_ref,
                    x1_ref, tok_ref, mi_ref, mg_ref, cnt_ref, carry):
    @pl.when(pl.program_id(0) == 0)
    def _():
        carry[...] = jnp.zeros_like(carry)

    feat = jnp.concatenate([att_ref[...].astype(BF16), lru_ref[...].astype(BF16),
                            four_ref[...].astype(BF16), ssd_ref[...].astype(BF16)], axis=1)
    acc = _dot(feat, wo_ref[...])
    x1 = x_ref[...] + mod_ref[2:3, :] * acc
    x1_ref[...] = x1
    tok = _rms(x1, n2_ref[...]) * (1.0 + mod_ref[4:5, :]) + mod_ref[3:4, :]
    tok_ref[...] = tok

    logits = _dot_hi(tok, rw_ref[...]) + rb_ref[...]
    tm = logits.shape[0]
    lane = lax.broadcasted_iota(I32, logits.shape, 1).astype(F32)
    work = logits
    vals, idxs = [], []
    for _ in range(TOP_K):
        m = jnp.max(work, axis=-1, keepdims=True)
        idx = jnp.min(jnp.where(work == m, lane, float(LANES)), axis=-1, keepdims=True)
        vals.append(m)
        idxs.append(idx)
        work = jnp.where(lane == idx, -jnp.inf, work)
    es = [jnp.exp(v - vals[0]) for v in vals]
    denom = es[0] + es[1] + es[2] + es[3]
    sel = jnp.zeros(logits.shape, F32)
    for idx in idxs:
        sel = sel + (lane == idx).astype(F32)
    ri = lax.broadcasted_iota(I32, (tm, tm), 0)
    ci = lax.broadcasted_iota(I32, (tm, tm), 1)
    strict_lower = (ci < ri).astype(BF16)
    cum = _dot(strict_lower, sel.astype(BF16)) + carry[...]
    carry[...] = carry[...] + jnp.sum(sel, axis=0, keepdims=True)
    cnt_ref[...] = jnp.broadcast_to(carry[...], cnt_ref.shape)
    mi = jnp.zeros(logits.shape, F32)
    mg = jnp.zeros(logits.shape, F32)
    for k in range(TOP_K):
        rank = jnp.sum(jnp.where(lane == idxs[k], cum, 0.0), axis=-1, keepdims=True)
        mi = jnp.where(lane == float(k), idxs[k], mi)
        mi = jnp.where(lane == float(TOP_K + k), rank, mi)
        mg = jnp.where(lane == float(k), es[k] / denom, mg)
    mi_ref[...] = mi.astype(I32)
    mg_ref[...] = mg


def _outproj(att, lru, four, ssd, wo4, x, mod6, norm2, rw, rb, dims):
    t, d = x.shape
    tm = math.gcd(math.gcd(dims["L"], dims["TC"]), TM_TOKEN)
    mod_row = dims["mod_row"]
    feat = pl.BlockSpec((tm, GROUP_W), lambda i: (i, 0))
    row_d = pl.BlockSpec((tm, d), lambda i: (i, 0))
    row_l = pl.BlockSpec((tm, LANES), lambda i: (i, 0))
    return pl.pallas_call(
        _outproj_kernel,
        out_shape=(jax.ShapeDtypeStruct((t, d), F32), jax.ShapeDtypeStruct((t, d), F32),
                   jax.ShapeDtypeStruct((t, LANES), I32), jax.ShapeDtypeStruct((t, LANES), F32),
                   jax.ShapeDtypeStruct((SUBLANES, LANES), F32)),
        grid=(t // tm,),
        in_specs=[feat, feat, feat, feat,
                  pl.BlockSpec(wo4.shape, lambda i: (0, 0)),
                  row_d,
                  pl.BlockSpec((None, 6, d), lambda i: (mod_row(i * tm), 0, 0)),
                  pl.BlockSpec((1, d), lambda i: (0, 0)),
                  pl.BlockSpec((d, LANES), lambda i: (0, 0)),
                  pl.BlockSpec((1, LANES), lambda i: (0, 0))],
        out_specs=(row_d, row_d, row_l, row_l, pl.BlockSpec((SUBLANES, LANES), lambda i: (0, 0))),
        scratch_shapes=[pltpu.VMEM((1, LANES), F32)],
        compiler_params=_cp(("arbitrary",)),
        name="outproj_router",
    )(att, lru, four, ssd, wo4, x, mod6, norm2.reshape(1, d), rw, rb)


def _gu_split_kernel(w_ref, p_ref, g_ref, u_ref):
    for c in range(g_ref.shape[1] // LANES):
        wb = w_ref[:, c * 2 * LANES:(c + 1) * 2 * LANES].astype(BF16)
        y = _dot(wb, p_ref[...])
        g_ref[:, c * LANES:(c + 1) * LANES] = y[:, :LANES].astype(BF16)
        u_ref[:, c * LANES:(c + 1) * LANES] = y[:, LANES:].astype(BF16)


def _gu_split(w_gu):
    g, d, ff2 = w_gu.shape
    ff = ff2 // 2
    tn = math.gcd(ff, 2 * LANES)
    perm = np.zeros((2 * LANES, 2 * LANES), np.float32)
    for j in range(LANES):
        perm[2 * j, j] = 1.0
        perm[2 * j + 1, LANES + j] = 1.0
    out = jax.ShapeDtypeStruct((g, d, ff), BF16)
    return pl.pallas_call(
        _gu_split_kernel,
        out_shape=(out, out),
        grid=(g, ff // tn),
        in_specs=[pl.BlockSpec((None, d, 2 * tn), lambda e, j: (e, 0, j)),
                  pl.BlockSpec((2 * LANES, 2 * LANES), lambda e, j: (0, 0))],
        out_specs=(pl.BlockSpec((None, d, tn), lambda e, j: (e, 0, j)),
                   pl.BlockSpec((None, d, tn), lambda e, j: (e, 0, j))),
        compiler_params=_cp(("arbitrary", "arbitrary")),
        name="gu_split",
    )(w_gu, jnp.asarray(perm, dtype=BF16))


ROWS_PER_ISSUE = 8


def _dispatch_kernel(zb_ref, dest_ref, tok_ref, xs_ref, zbuf, sem, zsem):
    td = tok_ref.shape[0]
    bm = zbuf.shape[0]

    @pl.when(pl.program_id(0) == 0)
    def _():
        zbuf[...] = jnp.zeros_like(zbuf)

        def zero_copy(z):
            start = pl.multiple_of(zb_ref[z] * bm, bm)
            return pltpu.make_async_copy(zbuf, xs_ref.at[pl.ds(start, bm)], zsem)

        def zstart(z, c):
            @pl.when(zb_ref[z] >= 0)
            def _():
                zero_copy(z).start()
            return c

        def zwait(z, c):
            @pl.when(zb_ref[z] >= 0)
            def _():
                zero_copy(z).wait()
            return c

        lax.fori_loop(0, zb_ref.shape[0], zstart, 0)
        lax.fori_loop(0, zb_ref.shape[0], zwait, 0)

    def issue(it, c):
        for rr in range(ROWS_PER_ISSUE):
            r = it * ROWS_PER_ISSUE + rr
            for k in range(TOP_K):
                pltpu.make_async_copy(tok_ref.at[pl.ds(r, 1)], xs_ref.at[pl.ds(dest_ref[r * TOP_K + k], 1)],
                                      sem).start(priority=k % 2)
        return c

    lax.fori_loop(0, td // ROWS_PER_ISSUE, issue, 0)
    for k in range(TOP_K):
        pltpu.make_async_copy(tok_ref, xs_ref.at[pl.ds(0, td)], sem).wait()


def _dispatch(tok, dest_flat, zero_blocks, p_rows):
    t, d = tok.shape
    td = math.gcd(t, 2 * TM_TOKEN)
    grid_spec = pltpu.PrefetchScalarGridSpec(
        num_scalar_prefetch=1,
        grid=(t // td,),
        in_specs=[
            pl.BlockSpec((td * TOP_K,), lambda i, zb: (i,), memory_space=pltpu.SMEM),
            pl.BlockSpec((td, d), lambda i, zb: (i, 0)),
        ],
        out_specs=pl.BlockSpec(memory_space=pl.ANY),
        scratch_shapes=[pltpu.VMEM((BM_MOE, d), F32), pltpu.SemaphoreType.DMA(()),
                        pltpu.SemaphoreType.DMA(())],
    )
    return pl.pallas_call(
        _dispatch_kernel,
        out_shape=jax.ShapeDtypeStruct((p_rows, d), F32),
        grid_spec=grid_spec,
        compiler_params=_cp(("arbitrary",)),
        name="moe_dispatch",
    )(zero_blocks, dest_flat, tok)


def _expert_kernel(be_ref, nu_ref, xs_ref, wg_ref, wu_ref, bg_ref, bu_ref, wd_ref, bd_ref, ys_ref):
    del be_ref
    i = pl.program_id(0)

    @pl.when(i < nu_ref[0])
    def _():
        x = xs_ref[...].astype(BF16)
        g = _dot(x, wg_ref[...]) + bg_ref[...]
        u = _dot(x, wu_ref[...]) + bu_ref[...]
        gate = jnp.minimum(g, SWIGLU_LIMIT)
        up = jnp.clip(u, -SWIGLU_LIMIT, SWIGLU_LIMIT)
        act = gate * _sigmoid(SWIGLU_ALPHA * gate) * (up + 1.0)
        ys_ref[...] = _dot(act.astype(BF16), wd_ref[...]) + bd_ref[...]

    @pl.when(i >= nu_ref[0])
    def _():
        ys_ref[...] = jnp.zeros_like(ys_ref)


def _experts(xs, blk_e, n_used, wg, wu, bg, bu, wd, bd):
    d = xs.shape[1]
    bm = BM_MOE
    p = blk_e.shape[0] * bm
    ff = wg.shape[2]

    def xrow(i, be, nu):
        return (jnp.minimum(i, nu[0] - 1), 0)

    def wsel(i, be, nu):
        return (be[i], 0, 0)

    grid_spec = pltpu.PrefetchScalarGridSpec(
        num_scalar_prefetch=2,
        grid=(p // bm,),
        in_specs=[
            pl.BlockSpec((bm, d), xrow),
            pl.BlockSpec((None, d, ff), wsel),
            pl.BlockSpec((None, d, ff), wsel),
            pl.BlockSpec((None, 1, ff), wsel),
            pl.BlockSpec((None, 1, ff), wsel),
            pl.BlockSpec((None, ff, d), wsel),
            pl.BlockSpec((None, 1, d), wsel),
        ],
        out_specs=pl.BlockSpec((bm, d), lambda i, be, nu: (i, 0)),
    )
    return pl.pallas_call(
        _expert_kernel,
        out_shape=jax.ShapeDtypeStruct((p, d), F32),
        grid_spec=grid_spec,
        compiler_params=_cp(("arbitrary",)),
        name="moe_experts",
    )(blk_e, n_used, xs, wg, wu, bg, bu, wd, bd)


def _combine_kernel(dcur_ref, dnext_ref, ys_ref, x1_ref, mg_ref, mod_ref, o_ref, buf, sem):
    i = pl.program_id(0)
    n = pl.num_programs(0)
    tm = x1_ref.shape[0]
    slot = i % 2

    def request(idx_ref, sl):
        def issue(it, c):
            for rr in range(ROWS_PER_ISSUE):
                r = it * ROWS_PER_ISSUE + rr
                for k in range(TOP_K):
                    pltpu.make_async_copy(ys_ref.at[pl.ds(idx_ref[r * TOP_K + k], 1)],
                                          buf.at[sl, k, pl.ds(r, 1)], sem.at[sl]).start(priority=k % 2)
            return c

        lax.fori_loop(0, tm // ROWS_PER_ISSUE, issue, 0)

    @pl.when(i == 0)
    def _():
        request(dcur_ref, 0)

    @pl.when(i + 1 < n)
    def _():
        request(dnext_ref, 1 - slot)

    for k in range(TOP_K):
        pltpu.make_async_copy(ys_ref.at[pl.ds(0, tm)], buf.at[slot, k], sem.at[slot]).wait()
    mg = mg_ref[...]
    ffn = mg[:, 0:1] * buf[slot, 0]
    for k in range(1, TOP_K):
        ffn = ffn + mg[:, k:k + 1] * buf[slot, k]
    o_ref[...] = x1_ref[...] + mod_ref[5:6, :] * ffn


def _combine(ys, dest_flat, x1, mg, mod6, dims):
    t, d = x1.shape
    tm = math.gcd(math.gcd(dims["L"], dims["TC"]), TM_TOKEN)
    mod_row = dims["mod_row"]
    nt = t // tm
    smem = functools.partial(pl.BlockSpec, memory_space=pltpu.SMEM)
    return pl.pallas_call(
        _combine_kernel,
        out_shape=jax.ShapeDtypeStruct((t, d), F32),
        grid=(nt,),
        in_specs=[
            smem((tm * TOP_K,), lambda i: (i,)),
            smem((tm * TOP_K,), lambda i: (jnp.minimum(i + 1, nt - 1),)),
            pl.BlockSpec(memory_space=pl.ANY),
            pl.BlockSpec((tm, d), lambda i: (i, 0)),
            pl.BlockSpec((tm, LANES), lambda i: (i, 0)),
            pl.BlockSpec((None, 6, d), lambda i: (mod_row(i * tm), 0, 0)),
        ],
        out_specs=pl.BlockSpec((tm, d), lambda i: (i, 0)),
        scratch_shapes=[pltpu.VMEM((2, TOP_K, tm, d), F32), pltpu.SemaphoreType.DMA((2,))],
        compiler_params=_cp(("arbitrary",)),
        name="moe_combine",
    )(dest_flat, dest_flat, ys, x1, mg, mod6)


def _rope_tables(seq, ctx):
    rows = seq // GRID_W
    row = np.repeat(np.arange(rows), GRID_W).astype(np.float32)
    col = np.tile(np.arange(GRID_W), rows).astype(np.float32)
    pairs = HEAD_DIM // 4
    inv = (ROPE_THETA ** (-np.arange(pairs, dtype=np.float32) / pairs)).astype(np.float32)
    ang = np.concatenate([row[:, None] * inv, col[:, None] * inv], axis=-1)
    ang = np.concatenate([ang, ang], axis=-1).astype(np.float32)
    sign = np.where(np.arange(HEAD_DIM) < HEAD_DIM // 2, -1.0, 1.0).astype(np.float32)
    cos = np.concatenate([np.ones((ctx, HEAD_DIM), np.float32), np.cos(ang)], axis=0)
    sin = np.concatenate([np.zeros((ctx, HEAD_DIM), np.float32), np.sin(ang) * sign], axis=0)
    return jnp.asarray(cos), jnp.asarray(sin)


def _dft_cos_sin(n):
    k = np.arange(n, dtype=np.int64)
    ang = 2.0 * np.pi * ((k[:, None] * k[None, :]) % n).astype(np.float64) / n
    return np.cos(ang), np.sin(ang)


def _seq_dft_matrix(n):
    c, s = _dft_cos_sin(n)
    return jnp.asarray(np.concatenate([c, -s], axis=1), dtype=BF16)


def _ssd_constants(direction):
    e = np.zeros((LANES, SSD_HEADS * SSD_HEAD_DIM), np.float32)
    for h in range(SSD_HEADS):
        e[direction * SSD_HEADS + h, h * SSD_HEAD_DIM:(h + 1) * SSD_HEAD_DIM] = 1.0
    li = np.arange(SSD_CHUNK)
    tri = (li[:, None] >= li[None, :]) if direction == 0 else (li[:, None] <= li[None, :])
    return jnp.asarray(e, dtype=BF16), jnp.asarray(tri.astype(np.float32), dtype=BF16)


def _lane_row(v):
    flat = v.reshape(1, -1).astype(F32)
    return jnp.pad(flat, ((0, 0), (0, LANES - flat.shape[1])))


def kernel(x, c, ctx, c_ctx, w_mod, b_mod, norm1, norm2, w_in, w_out, q_norm, k_norm, lru_conv_w, lru_conv_b,
           lru_wa, lru_ba, lru_wx, lru_bx, lru_lam, ssd_conv_w, ssd_conv_b, ssd_a_log, ssd_dt_bias, ssd_d,
           ssd_norm, router_w, router_b, exp_w_gu, exp_b_gu, exp_w_dn, exp_b_dn):
    b, seq, d = x.shape
    n_ctx = ctx.shape[1]
    depth = w_in.shape[0]
    n_exp = router_w.shape[2]
    ff = exp_w_dn.shape[2]
    nl, tc_rows = b * seq, b * n_ctx
    t = nl + tc_rows
    assert seq % SSD_CHUNK == 0 and n_ctx % SSD_CHUNK == 0 and seq % n_ctx == 0 and nl % n_ctx == 0
    assert seq % GRID_W == 0 and n_exp <= LANES and b + 1 <= 2 * SUBLANES

    def mod_row(r):
        return jnp.where(r < nl, r // seq, b)

    dims = dict(B=b, L=seq, C=n_ctx, NL=nl, TC=tc_rows, mod_row=mod_row)

    xa = jnp.concatenate([x.reshape(nl, d), ctx.reshape(tc_rows, d)], axis=0)
    cc = jnp.concatenate([c, c_ctx[None, :], jnp.zeros((2 * SUBLANES - b - 1, d), F32)], axis=0)
    mods = _mod_all(cc, w_mod, b_mod).reshape(depth, 2 * SUBLANES, 6, d)

    cos_t, sin_t = _rope_tables(seq, n_ctx)
    c128, s128 = _dft_cos_sin(GROUP_W // FN_GROUPS)
    c128, s128 = jnp.asarray(c128, dtype=BF16), jnp.asarray(s128, dtype=BF16)
    f_lat, f_ctx = _seq_dft_matrix(seq), _seq_dft_matrix(n_ctx)
    ssd_const = [_ssd_constants(0), _ssd_constants(1)]

    n_main = w_in.shape[2] - 2 * SSD_HEADS
    col_lx, col_lg, col_f, col_z, col_xbc = 1024, 1536, 2048, 2560, 3072
    xbc_w = n_main - col_xbc

    assert (t * TOP_K) % BM_MOE == 0
    n_blk = t * TOP_K // BM_MOE + n_exp
    wg_all, wu_all = _gu_split(exp_w_gu.reshape(depth * n_exp, d, 2 * ff))
    wg_all, wu_all = wg_all.reshape(depth, n_exp, d, ff), wu_all.reshape(depth, n_exp, d, ff)

    for l in range(depth):
        w_main = w_in[l, :, :n_main].astype(BF16)
        w_dt = jnp.pad(w_in[l, :, n_main:], ((0, 0), (0, LANES - 2 * SSD_HEADS))).astype(BF16)
        mod6 = mods[l]
        u, dt = _inproj(xa, mod6, norm1[l], w_main, w_dt, dims)

        qn, kn, vb = _qkprep(u, cos_t, sin_t, q_norm[l], k_norm[l], dims)
        att = _attention(qn, kn, vb, dims)

        lxc = _dwconv(u, col_lx, GROUP_W, lru_conv_w[l], lru_conv_b[l], dims, act=False)
        wa, wx = lru_wa[l].astype(BF16), lru_wx[l].astype(BF16)
        hb = _lru_pass(lxc, u, None, wa[1], lru_ba[l, 1], wx[1], lru_bx[l, 1], lru_lam[l, 1], None, dims, 1)
        lru = _lru_pass(lxc, u, col_lg // GROUP_W, wa[0], lru_ba[l, 0], wx[0], lru_bx[l, 0], lru_lam[l, 0],
                        hb, dims, 0)

        z = _chdft(u, col_f // GROUP_W, c128, s128, dims)
        four = _seqdft(f_lat, z, jnp.zeros((t, GROUP_W), F32), b, seq, 0, t)
        four = _seqdft(f_ctx, z, four, b, n_ctx, nl, t)

        xbc = _dwconv(u, col_xbc, xbc_w, ssd_conv_w[l], ssd_conv_b[l], dims, act=True)
        a_row = _lane_row(-jnp.exp(ssd_a_log[l]))
        bias_row = _lane_row(ssd_dt_bias[l])
        dsk = jnp.repeat(ssd_d[l], SSD_HEAD_DIM).reshape(1, -1)
        yf = _ssd_pass(xbc, dt, u, None, a_row, bias_row, *ssd_const[0], None, None, None, dims, 0)
        ssd = _ssd_pass(xbc, dt, u, col_z // GROUP_W, a_row, bias_row, *ssd_const[1], yf, dsk, ssd_norm[l],
                        dims, 1)

        wo4 = w_out[l].astype(BF16)
        rw = jnp.pad(router_w[l], ((0, 0), (0, LANES - n_exp)))
        rb = jnp.pad(router_b[l], (0, LANES - n_exp), constant_values=NEG_BIG).reshape(1, LANES)
        x1, tok, mi, mg, cnt = _outproj(att, lru, four, ssd, wo4, xa, mod6, norm2[l], rw, rb, dims)

        counts = cnt[0, :n_exp].astype(I32)
        pcounts = (counts + BM_MOE - 1) // BM_MOE * BM_MOE
        pends = jnp.cumsum(pcounts)
        pstarts = pends - pcounts
        dest = (pstarts[mi[:, :TOP_K]] + mi[:, TOP_K:2 * TOP_K]).reshape(-1)
        blk_start = jnp.arange(n_blk, dtype=I32) * BM_MOE
        blk_e = jnp.minimum(jnp.sum((pends[None, :] <= blk_start[:, None]).astype(I32), axis=1), n_exp - 1)
        n_used = (pends[-1:] // BM_MOE).astype(I32)
        trail = n_used[0] + jnp.arange(n_exp, dtype=I32)
        zero_blocks = jnp.concatenate([
            jnp.where(pcounts > 0, pends // BM_MOE - 1, -1),
            jnp.where(trail < n_blk, trail, -1)]).astype(I32)

        xs = _dispatch(tok, dest, zero_blocks, n_blk * BM_MOE)
        bgu = exp_b_gu[l].reshape(n_exp, 1, ff, 2)
        ys = _experts(xs, blk_e, n_used, wg_all[l], wu_all[l], bgu[..., 0], bgu[..., 1],
                      exp_w_dn[l].astype(BF16), exp_b_dn[l].reshape(n_exp, 1, d))
        xa = _combine(ys, dest, x1, mg, mod6, dims)

    return xa[:nl].reshape(b, seq, d)
```
